```python
import jax
import jax.numpy as jnp
from jax import lax
import numpy as np

D_MODEL = 1024
BATCH = 8
SEQ = 4096
DEPTH = 4

CTX_LEN = 256
GRID_W = 64
N_BRANCHES = 3
NA_HEADS = 8
NA_HEAD_DIM = 64
NA_WIDTH = NA_HEADS * NA_HEAD_DIM
NA_WIN_H = 8
NA_WIN_W = 16
GM_GROUPS = 4
GM_CHUNK = 128
GM_GROUP_DIM = 128
GM_WIDTH = GM_GROUPS * GM_GROUP_DIM
MLA_HEADS = 8
MLA_Q_RANK = 256
MLA_KV_RANK = 128
MLA_NOPE = 64
MLA_ROPE = 32
MLA_V = 64
MLA_QK = MLA_NOPE + MLA_ROPE
ROPE_AXIS_PAIRS = MLA_ROPE // 4
ROPE_THETA = 10000.0
FFN_HIDDEN = 4 * D_MODEL
Q_BLOCK = 128
NORM_EPS = 1e-6
NEG_INF = -1e30
MOD_SCALE = 0.5

OFF_NA_K = 0
OFF_NA_V = OFF_NA_K + NA_WIDTH
OFF_MLA_CKV = OFF_NA_V + NA_WIDTH
OFF_MLA_KR = OFF_MLA_CKV + MLA_KV_RANK
N_CTX_COLS = OFF_MLA_KR + MLA_ROPE
OFF_NA_Q = N_CTX_COLS
OFF_GM_U = OFF_NA_Q + NA_WIDTH
OFF_GM_V = OFF_GM_U + GM_WIDTH
OFF_MLA_CQ = OFF_GM_V + GM_WIDTH
OFF_GATES = OFF_MLA_CQ + MLA_Q_RANK
N_IN_COLS = OFF_GATES + N_BRANCHES * D_MODEL

kernel_name = 'hybrid_na_gmlp_mla_dit_block'


def rms_norm(t, gain):
    tf = t.astype(jnp.float32)
    y = tf * lax.rsqrt(jnp.mean(tf * tf, axis=-1, keepdims=True) + NORM_EPS)
    return (y * gain.astype(jnp.float32)).astype(t.dtype)


def layer_norm(t, gain, bias):
    tf = t.astype(jnp.float32)
    mu = jnp.mean(tf, axis=-1, keepdims=True)
    var = jnp.mean(jnp.square(tf - mu), axis=-1, keepdims=True)
    y = (tf - mu) * lax.rsqrt(var + NORM_EPS)
    return (y * gain.astype(jnp.float32) + bias.astype(jnp.float32)).astype(t.dtype)


def modulate(t, shift, scale):
    return t * (1 + scale) + shift


def cols(t, off, width):
    return t[..., off:off + width]


def split_heads(t, n_heads):
    b, n, w = t.shape
    return t.reshape(b, n, n_heads, w // n_heads).transpose(0, 2, 1, 3)


def merge_heads(t):
    b, h, n, d = t.shape
    return t.transpose(0, 2, 1, 3).reshape(b, n, h * d)


def axial_rope(n_tokens):
    t = jnp.arange(n_tokens)
    rows = (t // GRID_W).astype(jnp.float32)
    colsv = (t % GRID_W).astype(jnp.float32)
    freqs = ROPE_THETA ** (-jnp.arange(ROPE_AXIS_PAIRS, dtype=jnp.float32) / ROPE_AXIS_PAIRS)
    ang = jnp.stack([rows[:, None] * freqs, colsv[:, None] * freqs], axis=1)
    return jnp.cos(ang), jnp.sin(ang)


def apply_rope_tail(t, cos, sin):
    head, tail = t[..., :-MLA_ROPE], t[..., -MLA_ROPE:]
    tr = tail.reshape(tail.shape[:-1] + (2, 2, ROPE_AXIS_PAIRS)).astype(jnp.float32)
    x1, x2 = tr[..., 0, :], tr[..., 1, :]
    c, s = cos[:, None], sin[:, None]
    rot = jnp.stack([x1 * c - x2 * s, x2 * c + x1 * s], axis=-2).reshape(tail.shape).astype(t.dtype)
    return jnp.concatenate([head, rot], axis=-1)


def dense_attention(q, k, v):
    b, h, n, d = q.shape
    scale = d ** -0.5
    qb = jnp.moveaxis(q.reshape(b, h, n // Q_BLOCK, Q_BLOCK, d), 2, 0)

    def attend(q_blk):
        s = jnp.einsum('bhqd,bhkd->bhqk', q_blk, k).astype(jnp.float32) * scale
        p = jax.nn.softmax(s, axis=-1).astype(v.dtype)
        return jnp.einsum('bhqk,bhkd->bhqd', p, v)

    out = lax.map(attend, qb)
    return jnp.moveaxis(out, 0, 2).reshape(b, h, n, v.shape[-1])


def neighbourhood_attention(q, k, v, k_ctx, v_ctx, rpb):
    b, h, n, dh = q.shape
    rows = n // GRID_W
    kh = min(NA_WIN_H, rows)
    kw = NA_WIN_W
    scale = dh ** -0.5
    qg = jnp.moveaxis(q.reshape(b, h, rows, GRID_W, dh), 2, 0)
    kg = k.reshape(b, h, rows, GRID_W, dh)
    vg = v.reshape(b, h, rows, GRID_W, dh)
    r = jnp.arange(rows)
    r0 = jnp.clip(r - kh // 2, 0, rows - kh)
    cq = jnp.arange(GRID_W)
    c0 = jnp.clip(cq - kw // 2, 0, GRID_W - kw)
    col_in = (cq[None, :] >= c0[:, None]) & (cq[None, :] < c0[:, None] + kw)
    dc = jnp.clip(cq[None, :] - cq[:, None], -(NA_WIN_W - 1), NA_WIN_W - 1) + NA_WIN_W - 1

    def row_block(args):
        q_r, r_i, r0_i = args
        k_band = lax.dynamic_slice_in_dim(kg, r0_i, kh, axis=2)
        v_band = lax.dynamic_slice_in_dim(vg, r0_i, kh, axis=2)
        dr = r0_i + jnp.arange(kh) - r_i + NA_WIN_H - 1
        bias = rpb[:, dr[None, :, None], dc[:, None, :]].astype(jnp.float32)
        s_loc = jnp.einsum('bhqd,bhkcd->bhqkc', q_r, k_band).astype(jnp.float32) * scale + bias
        s_loc = jnp.where(col_in[:, None, :], s_loc, NEG_INF).reshape(b, h, GRID_W, kh * GRID_W)
        s_ctx = jnp.einsum('bhqd,bhld->bhql', q_r, k_ctx).astype(jnp.float32) * scale
        p = jax.nn.softmax(jnp.concatenate([s_loc, s_ctx], axis=-1), axis=-1).astype(v.dtype)
        p_loc = p[..., :kh * GRID_W].reshape(b, h, GRID_W, kh, GRID_W)
        return (jnp.einsum('bhqkc,bhkcd->bhqd', p_loc, v_band)
                + jnp.einsum('bhql,bhld->bhqd', p[..., kh * GRID_W:], v_ctx))

    out = lax.map(row_block, (qg, r, r0))
    return jnp.moveaxis(out, 0, 2).reshape(b, h, n, dh)


def spatial_gating(u, v, ln_g, ln_b, w_s, b_s):
    b, n, _ = v.shape
    v = layer_norm(v, ln_g, ln_b)
    vc = v.reshape(b, n // GM_CHUNK, GM_CHUNK, GM_GROUPS, GM_GROUP_DIM)
    mixed = jnp.einsum('gpq,bnqgc->bnpgc', w_s, vc) + b_s.T[:, :, None]
    return u * mixed.reshape(b, n, GM_WIDTH)


def mla_queries(c_q, cq_gain, w_uq, q_gain, rope):
    b, n, _ = c_q.shape
    q = (rms_norm(c_q, cq_gain) @ w_uq).reshape(b, n, MLA_HEADS, MLA_QK)
    q = rms_norm(q, q_gain)
    if rope is not None:
        q = apply_rope_tail(q, rope[0], rope[1])
    return q.transpose(0, 2, 1, 3)


def mla_keys_values(c_kv, k_rope, ckv_gain, w_ukv, k_gain, rope):
    b, n, _ = c_kv.shape
    kv = (rms_norm(c_kv, ckv_gain) @ w_ukv).reshape(b, n, MLA_HEADS, MLA_NOPE + MLA_V)
    k_nope, v = kv[..., :MLA_NOPE], kv[..., MLA_NOPE:]
    k_r = jnp.broadcast_to(k_rope[:, :, None, :], (b, n, MLA_HEADS, MLA_ROPE))
    k = rms_norm(jnp.concatenate([k_nope, k_r], axis=-1), k_gain)
    if rope is not None:
        k = apply_rope_tail(k, rope[0], rope[1])
    return k.transpose(0, 2, 1, 3), v.transpose(0, 2, 1, 3)


def merge_branches(y_na, y_gm, y_mla, gate_logits, na_w_o, gm_w_o, mla_w_o, w_out):
    g = jax.nn.sigmoid(gate_logits).reshape(gate_logits.shape[:-1] + (N_BRANCHES, D_MODEL))
    y = g[..., 0, :] * (y_na @ na_w_o) + g[..., 1, :] * (y_gm @ gm_w_o) + g[..., 2, :] * (y_mla @ mla_w_o)
    return y @ w_out


def squared_relu_mlp(h, w1, w2):
    return jnp.square(jax.nn.relu(h @ w1)) @ w2


def setup_inputs(seed: int = 0) -> dict:
    key = jax.random.key(seed)
    ks = jax.random.split(key, 28)
    f32 = jnp.float32

    def nrm(k, shape, scale):
        return scale * jax.random.normal(k, shape, f32)

    def gain(k, shape):
        return 1.0 + 0.02 * jax.random.normal(k, shape, f32)

    L = DEPTH
    return {
        'x': nrm(ks[0], (BATCH, SEQ, D_MODEL), 1.0),
        'c': nrm(ks[1], (BATCH, D_MODEL), 1.0),
        'ctx': nrm(ks[2], (BATCH, CTX_LEN, D_MODEL), 1.0),
        'c_ctx': nrm(ks[3], (D_MODEL,), 1.0),
        'w_mod': nrm(ks[4], (L, D_MODEL, 6 * D_MODEL), MOD_SCALE * D_MODEL ** -0.5),
        'b_mod': nrm(ks[5], (L, 6 * D_MODEL), 0.02),
        'g_norm1': gain(ks[6], (L, D_MODEL)),
        'g_norm2': gain(ks[7], (L, D_MODEL)),
        'w_in': nrm(ks[8], (L, D_MODEL, N_IN_COLS), D_MODEL ** -0.5),
        'na_q_gain': gain(ks[9], (L, NA_HEAD_DIM)),
        'na_k_gain': gain(ks[10], (L, NA_HEAD_DIM)),
        'na_rpb': nrm(ks[11], (L, NA_HEADS, 2 * NA_WIN_H - 1, 2 * NA_WIN_W - 1), 0.1),
        'na_w_o': nrm(ks[12], (L, NA_WIDTH, D_MODEL), NA_WIDTH ** -0.5),
        'gm_ln_g': gain(ks[13], (L, GM_WIDTH)),
        'gm_ln_b': nrm(ks[14], (L, GM_WIDTH), 0.02),
        'gm_w_s': nrm(ks[15], (L, GM_GROUPS, GM_CHUNK, GM_CHUNK), GM_CHUNK ** -0.5),
        'gm_b_s': gain(ks[16], (L, GM_GROUPS, GM_CHUNK)),
        'gm_w_o': nrm(ks[17], (L, GM_WIDTH, D_MODEL), GM_WIDTH ** -0.5),
        'mla_cq_gain': gain(ks[18], (L, MLA_Q_RANK)),
        'mla_ckv_gain': gain(ks[19], (L, MLA_KV_RANK)),
        'mla_w_uq': nrm(ks[20], (L, MLA_Q_RANK, MLA_HEADS * MLA_QK), MLA_Q_RANK ** -0.5),
        'mla_w_ukv': nrm(ks[21], (L, MLA_KV_RANK, MLA_HEADS * (MLA_NOPE + MLA_V)), MLA_KV_RANK ** -0.5),
        'mla_q_gain': gain(ks[22], (L, MLA_QK)),
        'mla_k_gain': gain(ks[23], (L, MLA_QK)),
        'mla_w_o': nrm(ks[24], (L, MLA_HEADS * MLA_V, D_MODEL), (MLA_HEADS * MLA_V) ** -0.5),
        'w_out': nrm(ks[25], (L, D_MODEL, D_MODEL), D_MODEL ** -0.5),
        'ffn_w1': nrm(ks[26], (L, D_MODEL, FFN_HIDDEN), D_MODEL ** -0.5),
        'ffn_w2': nrm(ks[27], (L, FFN_HIDDEN, D_MODEL), FFN_HIDDEN ** -0.5),
    }


def reference(x, c, ctx, c_ctx, w_mod, b_mod, g_norm1, g_norm2, w_in, na_q_gain, na_k_gain, na_rpb,
              na_w_o, gm_ln_g, gm_ln_b, gm_w_s, gm_b_s, gm_w_o, mla_cq_gain, mla_ckv_gain, mla_w_uq,
              mla_w_ukv, mla_q_gain, mla_k_gain, mla_w_o, w_out, ffn_w1, ffn_w2):
    rope = axial_rope(x.shape[1])
    s_c = jax.nn.silu(c)
    s_ctx = jax.nn.silu(c_ctx)[None]
    for i in range(DEPTH):
        last = i == DEPTH - 1
        mod = (s_c @ w_mod[i] + b_mod[i])[:, None, :]
        sh1, sc1, gt1, sh2, sc2, gt2 = jnp.split(mod, 6, axis=-1)
        n_mod_c = 2 * D_MODEL if last else 6 * D_MODEL
        mod_c = (s_ctx @ w_mod[i][:, :n_mod_c] + b_mod[i][:n_mod_c])[:, None, :]
        mc = jnp.split(mod_c, n_mod_c // D_MODEL, axis=-1)

        h = modulate(rms_norm(x, g_norm1[i]), sh1, sc1)
        hc = modulate(rms_norm(ctx, g_norm1[i]), mc[0], mc[1])
        p = h @ w_in[i]
        pc = hc @ w_in[i][:, :(N_CTX_COLS if last else N_IN_COLS)]

        k_na_c = rms_norm(split_heads(cols(pc, OFF_NA_K, NA_WIDTH), NA_HEADS), na_k_gain[i])
        v_na_c = split_heads(cols(pc, OFF_NA_V, NA_WIDTH), NA_HEADS)
        k_mla_c, v_mla_c = mla_keys_values(cols(pc, OFF_MLA_CKV, MLA_KV_RANK), cols(pc, OFF_MLA_KR, MLA_ROPE),
                                           mla_ckv_gain[i], mla_w_ukv[i], mla_k_gain[i], None)

        q_na = rms_norm(split_heads(cols(p, OFF_NA_Q, NA_WIDTH), NA_HEADS), na_q_gain[i])
        k_na = rms_norm(split_heads(cols(p, OFF_NA_K, NA_WIDTH), NA_HEADS), na_k_gain[i])
        v_na = split_heads(cols(p, OFF_NA_V, NA_WIDTH), NA_HEADS)
        y_na = merge_heads(neighbourhood_attention(q_na, k_na, v_na, k_na_c, v_na_c, na_rpb[i]))
        y_gm = spatial_gating(jax.nn.gelu(cols(p, OFF_GM_U, GM_WIDTH), approximate=False),
                              jax.nn.gelu(cols(p, OFF_GM_V, GM_WIDTH), approximate=False),
                              gm_ln_g[i], gm_ln_b[i], gm_w_s[i], gm_b_s[i])
        q_mla = mla_queries(cols(p, OFF_MLA_CQ, MLA_Q_RANK), mla_cq_gain[i], mla_w_uq[i], mla_q_gain[i], rope)
        k_mla, v_mla = mla_keys_values(cols(p, OFF_MLA_CKV, MLA_KV_RANK), cols(p, OFF_MLA_KR, MLA_ROPE),
                                       mla_ckv_gain[i], mla_w_ukv[i], mla_k_gain[i], rope)
        y_mla = merge_heads(dense_attention(q_mla, jnp.concatenate([k_mla_c, k_mla], axis=2),
                                            jnp.concatenate([v_mla_c, v_mla], axis=2)))
        mix = merge_branches(y_na, y_gm, y_mla, cols(p, OFF_GATES, N_BRANCHES * D_MODEL),
                             na_w_o[i], gm_w_o[i], mla_w_o[i], w_out[i])
        x_new = x + gt1 * mix
        x_new = x_new + gt2 * squared_relu_mlp(modulate(rms_norm(x_new, g_norm2[i]), sh2, sc2),
                                               ffn_w1[i], ffn_w2[i])

        if not last:
            q_na_c = rms_norm(split_heads(cols(pc, OFF_NA_Q, NA_WIDTH), NA_HEADS), na_q_gain[i])
            y_na_c = merge_heads(dense_attention(q_na_c, k_na_c, v_na_c))
            y_gm_c = spatial_gating(jax.nn.gelu(cols(pc, OFF_GM_U, GM_WIDTH), approximate=False),
                                    jax.nn.gelu(cols(pc, OFF_GM_V, GM_WIDTH), approximate=False),
                                    gm_ln_g[i], gm_ln_b[i], gm_w_s[i], gm_b_s[i])
            q_mla_c = mla_queries(cols(pc, OFF_MLA_CQ, MLA_Q_RANK), mla_cq_gain[i], mla_w_uq[i],
                                  mla_q_gain[i], None)
            y_mla_c = merge_heads(dense_attention(q_mla_c, k_mla_c, v_mla_c))
            mix_c = merge_branches(y_na_c, y_gm_c, y_mla_c, cols(pc, OFF_GATES, N_BRANCHES * D_MODEL),
                                   na_w_o[i], gm_w_o[i], mla_w_o[i], w_out[i])
            ctx = ctx + mc[2] * mix_c
            ctx = ctx + mc[5] * squared_relu_mlp(modulate(rms_norm(ctx, g_norm2[i]), mc[3], mc[4]),
                                                 ffn_w1[i], ffn_w2[i])
        x = x_new
    return x
```

```python
import functools
import math

import numpy as np
import jax
import jax.numpy as jnp
from jax import lax
from jax.experimental import pallas as pl
from jax.experimental.pallas import tpu as pltpu

F32 = jnp.float32
BF16 = jnp.bfloat16

LANES = 128
VMEM_LIMIT_BYTES = 60 * 1024 * 1024

GRID_W = 64
N_BRANCHES = 3
NA_HEADS = 8
NA_HEAD_DIM = 64
NA_WIDTH = NA_HEADS * NA_HEAD_DIM
NA_WIN_H = 8
NA_WIN_W = 16
GM_GROUPS = 4
GM_CHUNK = 128
GM_GROUP_DIM = 128
GM_WIDTH = GM_GROUPS * GM_GROUP_DIM
MLA_HEADS = 8
MLA_Q_RANK = 256
MLA_KV_RANK = 128
MLA_NOPE = 64
MLA_ROPE = 32
MLA_V = 64
MLA_QK = MLA_NOPE + MLA_ROPE
ROPE_AXIS_PAIRS = MLA_ROPE // 4
ROPE_THETA = 10000.0
NORM_EPS = 1e-6
NEG_INF = -1e30
LOG2E = math.log2(math.e)

NA_QR = 4
NA_BAND = 12
MLA_SLOT = LANES


def _resident(shape):
    zeros = (0,) * len(shape)
    return pl.BlockSpec(shape, lambda *_: zeros, pipeline_mode=pl.Buffered(1))


def _params():
    return pltpu.CompilerParams(vmem_limit_bytes=VMEM_LIMIT_BYTES)


def _dot(a, b):
    return jnp.dot(a, b, preferred_element_type=F32)


def _dot_t(a, b):
    return lax.dot_general(a, b, (((1,), (1,)), ((), ())), preferred_element_type=F32)


def _mod_kernel(s_ref, w_ref, b_ref, o_ref):
    s = s_ref[...]
    s = s * jax.nn.sigmoid(s)
    o_ref[...] = _dot(s.astype(BF16), w_ref[...].astype(BF16)) + b_ref[...]


def _modulation(rows, w_mod, b_mod):
    depth, d, n = w_mod.shape
    r = rows.shape[0]
    tn = 1536
    return pl.pallas_call(
        _mod_kernel,
        grid=(depth, n // tn),
        in_specs=[
            pl.BlockSpec((r, d), lambda l, j: (0, 0)),
            pl.BlockSpec((None, d, tn), lambda l, j: (l, 0, j)),
            pl.BlockSpec((None, 1, tn), lambda l, j: (l, 0, j)),
        ],
        out_specs=pl.BlockSpec((None, r, tn), lambda l, j: (l, 0, j)),
        out_shape=jax.ShapeDtypeStruct((depth, r, n), F32),
        compiler_params=_params(),
        name="modulation",
    )(rows, w_mod, b_mod.reshape(depth, 1, n))


C_NAQ = 0
C_NAK = C_NAQ + NA_WIDTH
C_NAV = C_NAK + NA_WIDTH
C_CKV = C_NAV + NA_WIDTH
C_KR = C_CKV + MLA_KV_RANK
C_CQ = C_KR + LANES
C_GMU = C_CQ + MLA_Q_RANK
C_GMV = C_GMU + GM_WIDTH
C_GATES = C_GMV + GM_WIDTH
C_TOTAL = C_GATES + N_BRANCHES * 1024


def _rms(t, width):
    return t * lax.rsqrt(jnp.sum(t * t, axis=-1, keepdims=True) * (1.0 / width) + NORM_EPS)


def _gelu(t):
    return 0.5 * t * (1.0 + lax.erf(t * math.sqrt(0.5)))


def _head64_norm(p, bd):
    p2 = (p * p).astype(BF16)
    half = 2 * LANES
    ss = jnp.concatenate([_dot(p2[:, :half], bd), _dot(p2[:, half:], bd)], axis=1)
    return p * lax.rsqrt(ss * (1.0 / NA_HEAD_DIM) + NORM_EPS)


def _inproj_kernel(x_ref, sh_ref, sc_ref, gn_ref, w_ref, qg_ref, kg_ref, bd_ref, ckvg_ref, cqg_ref, lng_ref,
                   lnb_ref, wq_ref, wqp_ref, gq_ref, gqp_ref, wk_ref, wkp_ref, gk_ref, gkp_ref, wv_ref, cos_ref,
                   sin_ref, naq_ref, nak_ref, nav_ref, gmu_ref, gmv_ref, gates_ref, qm_ref, km_ref, vm_ref):
    x = x_ref[...]
    d = x.shape[-1]
    h = _rms(x, d) * gn_ref[...]
    h = h * (1.0 + sc_ref[...]) + sh_ref[...]
    hb = h.astype(BF16)

    def proj(c0, width):
        return _dot(hb, w_ref[:, c0:c0 + width])

    bd = bd_ref[...]
    naq_ref[...] = (_head64_norm(proj(C_NAQ, NA_WIDTH), bd) * qg_ref[...]).astype(BF16)
    nak_ref[...] = (_head64_norm(proj(C_NAK, NA_WIDTH), bd) * kg_ref[...]).astype(BF16)
    nav_ref[...] = proj(C_NAV, NA_WIDTH).astype(BF16)

    gmu_ref[...] = _gelu(proj(C_GMU, GM_WIDTH)).astype(BF16)
    gv = _gelu(proj(C_GMV, GM_WIDTH))
    mu = jnp.mean(gv, axis=-1, keepdims=True)
    gc = gv - mu
    var = jnp.mean(gc * gc, axis=-1, keepdims=True)
    gmv_ref[...] = (gc * lax.rsqrt(var + NORM_EPS) * lng_ref[...] + lnb_ref[...]).astype(BF16)

    n_gate = gates_ref.shape[-1]
    gate_chunk = 512
    for j in range(n_gate // gate_chunk):
        cs = slice(j * gate_chunk, (j + 1) * gate_chunk)
        gates_ref[:, cs] = jax.nn.sigmoid(proj(C_GATES + j * gate_chunk, gate_chunk)).astype(BF16)

    ckv_n = (_rms(proj(C_CKV, MLA_KV_RANK), MLA_KV_RANK) * ckvg_ref[...]).astype(BF16)
    cq_n = (_rms(proj(C_CQ, MLA_Q_RANK), MLA_Q_RANK) * cqg_ref[...]).astype(BF16)
    kr = proj(C_KR, LANES)
    kr_hi = kr.astype(BF16)
    kr_lo = (kr - kr_hi.astype(F32)).astype(BF16)
    lane = lax.broadcasted_iota(jnp.int32, (1, LANES), 1)
    kr_hl = jnp.where(lane < MLA_ROPE, kr_hi, kr_lo)
    lhs_k = jnp.concatenate([ckv_n, kr_hl], axis=1)

    vm_ref[...] = _dot(ckv_n, wv_ref[...]).astype(BF16)

    cosv = cos_ref[...]
    sinv = sin_ref[...]

    def heads(lhs, w_r, wp_r, g_r, gp_r, out_r):
        xa = _dot(lhs, w_r[...])
        xpa = _dot(lhs, wp_r[...])
        for hd in range(MLA_HEADS):
            sl = slice(hd * MLA_SLOT, (hd + 1) * MLA_SLOT)
            xh = xa[:, sl]
            r = lax.rsqrt(jnp.sum(xh * xh, axis=-1, keepdims=True) * (1.0 / MLA_QK) + NORM_EPS)
            out_r[:, sl] = (r * (xh * (g_r[:, sl] * cosv) + xpa[:, sl] * (gp_r[:, sl] * sinv))).astype(BF16)

    heads(cq_n, wq_ref, wqp_ref, gq_ref, gqp_ref, qm_ref)
    heads(lhs_k, wk_ref, wkp_ref, gk_ref, gkp_ref, km_ref)


def _in_projection(xs, mod_l, mod_row_fn, tm, tiles_per_seq, gn, lw, cos_t, sin_t):
    r, d = xs.shape
    n_tiles = r // tm
    row = lambda w: pl.BlockSpec((tm, w), lambda i: (i, 0))
    modspec = lambda chunk: pl.BlockSpec((None, None, 1, d), lambda i: (mod_row_fn(i), chunk, 0, 0))
    pos = pl.BlockSpec((tm, LANES), lambda i: (i % tiles_per_seq, 0))
    in_specs = [
        row(d), modspec(0), modspec(1), _resident((1, d)), _resident((d, C_TOTAL)),
        _resident((1, NA_WIDTH)), _resident((1, NA_WIDTH)), _resident((2 * LANES, 2 * LANES)),
        _resident((1, MLA_KV_RANK)), _resident((1, MLA_Q_RANK)), _resident((1, GM_WIDTH)), _resident((1, GM_WIDTH)),
        _resident((MLA_Q_RANK, MLA_HEADS * MLA_SLOT)), _resident((MLA_Q_RANK, MLA_HEADS * MLA_SLOT)),
        _resident((1, MLA_HEADS * MLA_SLOT)), _resident((1, MLA_HEADS * MLA_SLOT)),
        _resident((2 * LANES, MLA_HEADS * MLA_SLOT)), _resident((2 * LANES, MLA_HEADS * MLA_SLOT)),
        _resident((1, MLA_HEADS * MLA_SLOT)), _resident((1, MLA_HEADS * MLA_SLOT)),
        _resident((MLA_KV_RANK, MLA_HEADS * MLA_V)), pos, pos,
    ]
    widths = [NA_WIDTH, NA_WIDTH, NA_WIDTH, GM_WIDTH, GM_WIDTH, N_BRANCHES * d, MLA_HEADS * MLA_SLOT,
              MLA_HEADS * MLA_SLOT, MLA_HEADS * MLA_V]
    return pl.pallas_call(
        _inproj_kernel,
        grid=(n_tiles,),
        in_specs=in_specs,
        out_specs=[row(w) for w in widths],
        out_shape=[jax.ShapeDtypeStruct((r, w), BF16) for w in widths],
        compiler_params=_params(),
        name="in_projection",
    )(xs, mod_l, mod_l, gn, lw["w_in"], lw["na_qg"], lw["na_kg"], lw["bd"], lw["ckv_g"], lw["cq_g"], lw["ln_g"],
      lw["ln_b"], lw["wq"], lw["wqp"], lw["gq"], lw["gqp"], lw["wk"], lw["wkp"], lw["gk"], lw["gkp"], lw["wv"],
      cos_t, sin_t)


def _na_kernel(q_ref, k_ref, v_ref, kc_ref, vc_ref, bias_ref, o_ref, *, grid_rows):
    rb = pl.program_id(1)
    band0 = jnp.clip(NA_QR * rb - NA_WIN_H // 2, 0, grid_rows - NA_BAND)
    k0 = pl.multiple_of(band0 * GRID_W, GRID_W)
    nk = NA_BAND * GRID_W
    lane = lax.broadcasted_iota(jnp.int32, (1, LANES), 1)
    for hp in range(NA_HEADS // 2):
        cs = slice(hp * LANES, (hp + 1) * LANES)
        qp = q_ref[:, cs]
        kb = k_ref[pl.ds(k0, nk), cs]
        vb = v_ref[pl.ds(k0, nk), cs]
        kc = kc_ref[:, cs]
        vc = vc_ref[:, cs]
        out = None
        for hh in range(2):
            in_head = (lane < NA_HEAD_DIM) if hh == 0 else (lane >= NA_HEAD_DIM)
            qh = jnp.where(in_head, qp, jnp.zeros_like(qp))
            s_loc = _dot_t(qh, kb) + bias_ref[0, 2 * hp + hh]
            s_ctx = _dot_t(qh, kc)
            m = jnp.maximum(jnp.max(s_loc, axis=-1, keepdims=True), jnp.max(s_ctx, axis=-1, keepdims=True))
            p_loc = jnp.exp(s_loc - m)
            p_ctx = jnp.exp(s_ctx - m)
            l = jnp.sum(p_loc, axis=-1, keepdims=True) + jnp.sum(p_ctx, axis=-1, keepdims=True)
            vbh = jnp.where(in_head, vb, jnp.zeros_like(vb))
            vch = jnp.where(in_head, vc, jnp.zeros_like(vc))
            o = (_dot(p_loc.astype(BF16), vbh) + _dot(p_ctx.astype(BF16), vch)) / l
            out = o if out is None else out + o
        o_ref[:, cs] = out.astype(BF16)


def _na_attention(q, k, v, kc, vc, bias, batch, seq, ctx_len):
    grid_rows = seq // GRID_W
    n_rb = grid_rows // NA_QR
    tq = NA_QR * GRID_W
    nk = NA_BAND * GRID_W

    def bias_cls(b, rb):
        return (jnp.where(rb == 0, 0, jnp.where(rb == n_rb - 1, 2, 1)), 0, 0, 0)

    return pl.pallas_call(
        functools.partial(_na_kernel, grid_rows=grid_rows),
        grid=(batch, n_rb),
        in_specs=[
            pl.BlockSpec((tq, NA_WIDTH), lambda b, rb: (b * n_rb + rb, 0)),
            pl.BlockSpec((seq, NA_WIDTH), lambda b, rb: (b, 0)),
            pl.BlockSpec((seq, NA_WIDTH), lambda b, rb: (b, 0)),
            pl.BlockSpec((ctx_len, NA_WIDTH), lambda b, rb: (b, 0)),
            pl.BlockSpec((ctx_len, NA_WIDTH), lambda b, rb: (b, 0)),
            pl.BlockSpec((1, NA_HEADS, tq, nk), bias_cls),
        ],
        out_specs=pl.BlockSpec((tq, NA_WIDTH), lambda b, rb: (b * n_rb + rb, 0)),
        out_shape=jax.ShapeDtypeStruct((batch * seq, NA_WIDTH), BF16),
        compiler_params=_params(),
        name="na_attention",
    )(q, k, v, kc, vc, bias)


def _na_bias_table(rpb, grid_rows):
    n_rb = grid_rows // NA_QR
    kh, kw = NA_WIN_H, NA_WIN_W
    cq = np.arange(GRID_W)
    c0 = np.clip(cq - kw // 2, 0, GRID_W - kw)
    col_in = (cq[None, :] >= c0[:, None]) & (cq[None, :] < c0[:, None] + kw)
    dc = np.clip(cq[None, :] - cq[:, None], -(kw - 1), kw - 1) + kw - 1
    dr = np.zeros((3, NA_QR, NA_BAND), np.int32)
    ok = np.zeros((3, NA_QR, NA_BAND), bool)
    for c, rb in enumerate((0, 1, n_rb - 1)):
        band0 = int(np.clip(NA_QR * rb - kh // 2, 0, grid_rows - NA_BAND))
        for i in range(NA_QR):
            rq = NA_QR * rb + i
            r0 = int(np.clip(rq - kh // 2, 0, grid_rows - kh))
            for j in range(NA_BAND):
                kr = band0 + j
                ok[c, i, j] = r0 <= kr < r0 + kh
                dr[c, i, j] = int(np.clip(kr - rq + kh - 1, 0, 2 * kh - 2))
    t = rpb[:, dr.reshape(-1), :][:, :, dc]
    t = t.reshape(NA_HEADS, 3, NA_QR, NA_BAND, GRID_W, GRID_W)
    mask = ok[None, :, :, :, None, None] & col_in[None, None, None, None, :, :]
    t = jnp.where(mask, t, NEG_INF)
    t = t.transpose(1, 0, 2, 4, 3, 5)
    return t.reshape(3, NA_HEADS, NA_QR * GRID_W, NA_BAND * GRID_W).astype(F32)


def _mla_kernel(q_ref, kc_ref, vc_ref, *rest, n_kb, tk):
    if n_kb:
        k_ref, v_ref, o_ref = rest
    else:
        (o_ref,) = rest
    lane = lax.broadcasted_iota(jnp.int32, (1, LANES), 1)
    out = None
    for hh in range(2):
        in_head = (lane < MLA_V) if hh == 0 else (lane >= MLA_V)
        hs = slice(hh * MLA_SLOT, (hh + 1) * MLA_SLOT)
        qh = q_ref[:, hs]
        vcp = vc_ref[...]
        s = _dot_t(qh, kc_ref[:, hs])
        m = jnp.max(s, axis=-1, keepdims=True)
        p = jnp.exp2(s - m)
        l = jnp.sum(p, axis=-1, keepdims=True)
        acc = _dot(p.astype(BF16), jnp.where(in_head, vcp, jnp.zeros_like(vcp)))

        if n_kb:
            def step(kb, carry, hs=hs, in_head=in_head, qh=qh):
                m, l, acc = carry
                r0 = pl.multiple_of(kb * tk, tk)
                s = _dot_t(qh, k_ref[pl.ds(r0, tk), hs])
                vb = v_ref[pl.ds(r0, tk), :]
                m_new = jnp.maximum(m, jnp.max(s, axis=-1, keepdims=True))
                alpha = jnp.exp2(m - m_new)
                p = jnp.exp2(s - m_new)
                l = alpha * l + jnp.sum(p, axis=-1, keepdims=True)
                acc = alpha * acc + _dot(p.astype(BF16), jnp.where(in_head, vb, jnp.zeros_like(vb)))
                return m_new, l, acc

            m, l, acc = lax.fori_loop(0, n_kb, step, (m, l, acc))
        o = acc / l
        out = o if out is None else out + o
    o_ref[...] = out.astype(BF16)


def _mla_attention(q, kc, vc, k, v, batch, q_len, ctx_len, kv_len, tq, tk):
    n_q = q_len // tq
    n_pairs = MLA_HEADS // 2
    n_kb = kv_len // tk if k is not None else 0
    in_specs = [
        pl.BlockSpec((tq, 2 * MLA_SLOT), lambda b, p, i: (b * n_q + i, p)),
        pl.BlockSpec((ctx_len, 2 * MLA_SLOT), lambda b, p, i: (b, p)),
        pl.BlockSpec((ctx_len, LANES), lambda b, p, i: (b, p)),
    ]
    args = [q, kc, vc]
    if n_kb:
        in_specs += [
            pl.BlockSpec((kv_len, 2 * MLA_SLOT), lambda b, p, i: (b, p)),
            pl.BlockSpec((kv_len, LANES), lambda b, p, i: (b, p)),
        ]
        args += [k, v]
    return pl.pallas_call(
        functools.partial(_mla_kernel, n_kb=n_kb, tk=tk),
        grid=(batch, n_pairs, n_q),
        in_specs=in_specs,
        out_specs=pl.BlockSpec((tq, LANES), lambda b, p, i: (b * n_q + i, p)),
        out_shape=jax.ShapeDtypeStruct((batch * q_len, MLA_HEADS * MLA_V), BF16),
        compiler_params=_params(),
        name="mla_attention" if n_kb else "mla_ctx_attention",
    )(*args)


def _na_ctx_kernel(q_ref, k_ref, v_ref, o_ref):
    lane = lax.broadcasted_iota(jnp.int32, (1, LANES), 1)
    for hp in range(NA_HEADS // 2):
        cs = slice(hp * LANES, (hp + 1) * LANES)
        qp, kp, vp = q_ref[:, cs], k_ref[:, cs], v_ref[:, cs]
        out = None
        for hh in range(2):
            in_head = (lane < NA_HEAD_DIM) if hh == 0 else (lane >= NA_HEAD_DIM)
            s = _dot_t(jnp.where(in_head, qp, jnp.zeros_like(qp)), kp)
            m = jnp.max(s, axis=-1, keepdims=True)
            p = jnp.exp(s - m)
            l = jnp.sum(p, axis=-1, keepdims=True)
            o = _dot(p.astype(BF16), jnp.where(in_head, vp, jnp.zeros_like(vp))) / l
            out = o if out is None else out + o
        o_ref[:, cs] = out.astype(BF16)


def _na_ctx_attention(q, k, v, batch, ctx_len):
    spec = pl.BlockSpec((ctx_len, NA_WIDTH), lambda b: (b, 0))
    return pl.pallas_call(
        _na_ctx_kernel,
        grid=(batch,),
        in_specs=[spec, spec, spec],
        out_specs=spec,
        out_shape=jax.ShapeDtypeStruct((batch * ctx_len, NA_WIDTH), BF16),
        compiler_params=_params(),
        name="na_ctx_attention",
    )(q, k, v)


FFN_HIDDEN_CHUNK = 512


def _merge_ffn_kernel(x_ref, yna_ref, ymla_ref, gmu_ref, gmv_ref, gates_ref, gt1_ref, sh2_ref, sc2_ref, gt2_ref,
                      gn2_ref, wna_ref, wgm_ref, wmla_ref, wout_ref, w1_ref, w2_ref, ws_ref, bsb_ref, o_ref,
                      ygm_ref):
    tm, d = x_ref.shape
    for c in range(tm // GM_CHUNK):
        rs = slice(c * GM_CHUNK, (c + 1) * GM_CHUNK)
        for g in range(GM_GROUPS):
            cs = slice(g * GM_GROUP_DIM, (g + 1) * GM_GROUP_DIM)
            mixed = _dot(ws_ref[g], gmv_ref[rs, cs]) + bsb_ref[:, cs]
            ygm_ref[rs, cs] = (gmu_ref[rs, cs].astype(F32) * mixed).astype(BF16)

    y = gates_ref[:, 0:d].astype(F32) * _dot(yna_ref[...], wna_ref[...])
    y = y + gates_ref[:, d:2 * d].astype(F32) * _dot(ygm_ref[...], wgm_ref[...])
    y = y + gates_ref[:, 2 * d:3 * d].astype(F32) * _dot(ymla_ref[...], wmla_ref[...])
    xn = x_ref[...] + gt1_ref[...] * _dot(y.astype(BF16), wout_ref[...])

    h2 = _rms(xn, d) * gn2_ref[...]
    h2 = (h2 * (1.0 + sc2_ref[...]) + sh2_ref[...]).astype(BF16)
    hidden = w1_ref.shape[1]
    acc = None
    for c in range(hidden // FFN_HIDDEN_CHUNK):
        cs = slice(c * FFN_HIDDEN_CHUNK, (c + 1) * FFN_HIDDEN_CHUNK)
        a = jnp.maximum(_dot(h2, w1_ref[:, cs]), 0.0)
        part = _dot((a * a).astype(BF16), w2_ref[cs, :])
        acc = part if acc is None else acc + part
    o_ref[...] = xn + gt2_ref[...] * acc


def _merge_ffn(xs, y_na, y_mla, gm_u, gm_v, gates, mod_l, mod_row_fn, tm, gn2, lw):
    r, d = xs.shape
    hidden = lw["w1"].shape[1]
    row = lambda w: pl.BlockSpec((tm, w), lambda i: (i, 0))
    modspec = lambda chunk: pl.BlockSpec((None, None, 1, d), lambda i: (mod_row_fn(i), chunk, 0, 0))
    in_specs = [
        row(d), row(NA_WIDTH), row(MLA_HEADS * MLA_V), row(GM_WIDTH), row(GM_WIDTH), row(N_BRANCHES * d),
        modspec(2), modspec(3), modspec(4), modspec(5), _resident((1, d)),
        _resident((NA_WIDTH, d)), _resident((GM_WIDTH, d)), _resident((MLA_HEADS * MLA_V, d)), _resident((d, d)),
        _resident((d, hidden)), _resident((hidden, d)), _resident((GM_GROUPS, GM_CHUNK, GM_CHUNK)),
        _resident((GM_CHUNK, GM_WIDTH)),
    ]
    return pl.pallas_call(
        _merge_ffn_kernel,
        grid=(r // tm,),
        in_specs=in_specs,
        out_specs=row(d),
        out_shape=jax.ShapeDtypeStruct((r, d), F32),
        scratch_shapes=[pltpu.VMEM((tm, GM_WIDTH), BF16)],
        compiler_params=_params(),
        name="merge_ffn",
    )(xs, y_na, y_mla, gm_u, gm_v, gates, mod_l, mod_l, mod_l, mod_l, gn2, lw["w_na_o"], lw["w_gm_o"],
      lw["w_mla_o"], lw["w_out"], lw["w1"], lw["w2"], lw["w_s"], lw["b_s"])


def _rope_partner():
    perm = np.arange(MLA_QK)
    for u in range(MLA_ROPE):
        seg, half, pair = u // (2 * ROPE_AXIS_PAIRS), (u // ROPE_AXIS_PAIRS) % 2, u % ROPE_AXIS_PAIRS
        perm[MLA_NOPE + u] = MLA_NOPE + seg * 2 * ROPE_AXIS_PAIRS + (1 - half) * ROPE_AXIS_PAIRS + pair
    tail = np.arange(MLA_QK) >= MLA_NOPE
    return perm, tail


def _pad_heads(t):
    pad = [(0, 0)] * (t.ndim - 1) + [(0, MLA_SLOT - MLA_QK)]
    t = jnp.pad(t, pad)
    return t.reshape(t.shape[:-2] + (MLA_HEADS * MLA_SLOT,))


def _rope_tables(seq):
    t = jnp.arange(seq)
    rows = (t // GRID_W).astype(F32)
    colsv = (t % GRID_W).astype(F32)
    freqs = ROPE_THETA ** (-jnp.arange(ROPE_AXIS_PAIRS, dtype=F32) / ROPE_AXIS_PAIRS)
    ang = jnp.stack([rows[:, None] * freqs, colsv[:, None] * freqs], axis=1)
    cos, sin = jnp.cos(ang), jnp.sin(ang)
    cos_t = jnp.stack([cos, cos], axis=2).reshape(seq, MLA_ROPE)
    sin_t = jnp.stack([-sin, sin], axis=2).reshape(seq, MLA_ROPE)
    ones = jnp.ones((seq, MLA_NOPE), F32)
    padc = jnp.ones((seq, MLA_SLOT - MLA_QK), F32)
    cos_full = jnp.concatenate([ones, cos_t, padc], axis=1)
    sin_full = jnp.concatenate([0 * ones, sin_t, 0 * padc], axis=1)
    return cos_full, sin_full


def _pack_layer(i, p, off):
    d = p["w_in"].shape[1]
    w = p["w_in"][i]
    kr = w[:, off["kr"]:off["kr"] + MLA_ROPE]
    w_in = jnp.concatenate([
        w[:, off["naq"]:off["naq"] + NA_WIDTH], w[:, off["nak"]:off["nak"] + NA_WIDTH],
        w[:, off["nav"]:off["nav"] + NA_WIDTH], w[:, off["ckv"]:off["ckv"] + MLA_KV_RANK],
        kr, kr, jnp.zeros((d, LANES - 2 * MLA_ROPE), w.dtype),
        w[:, off["cq"]:off["cq"] + MLA_Q_RANK], w[:, off["gmu"]:off["gmu"] + GM_WIDTH],
        w[:, off["gmv"]:off["gmv"] + GM_WIDTH], w[:, off["gates"]:off["gates"] + N_BRANCHES * d],
    ], axis=1).astype(BF16)

    perm, tail = _rope_partner()
    q_scale = MLA_QK ** -0.5 * LOG2E
    w_uq = p["mla_w_uq"][i].reshape(MLA_Q_RANK, MLA_HEADS, MLA_QK)
    wq = _pad_heads(w_uq).astype(BF16)
    wqp = _pad_heads(w_uq[:, :, perm] * tail).astype(BF16)
    qg = p["mla_q_gain"][i]
    gq = jnp.tile(jnp.pad(qg, (0, MLA_SLOT - MLA_QK)), MLA_HEADS)[None] * q_scale
    gqp = jnp.tile(jnp.pad(qg[perm] * tail, (0, MLA_SLOT - MLA_QK)), MLA_HEADS)[None] * q_scale

    w_ukv = p["mla_w_ukv"][i].reshape(MLA_KV_RANK, MLA_HEADS, MLA_NOPE + MLA_V)
    k_nope = jnp.pad(w_ukv[:, :, :MLA_NOPE], ((0, 0), (0, 0), (0, MLA_SLOT - MLA_NOPE)))
    k_nope = k_nope.reshape(MLA_KV_RANK, MLA_HEADS * MLA_SLOT)
    sel = np.zeros((LANES, MLA_HEADS, MLA_SLOT), np.float32)
    selp = np.zeros((LANES, MLA_HEADS, MLA_SLOT), np.float32)
    for t in range(MLA_ROPE):
        for copy in range(2):
            sel[copy * MLA_ROPE + t, :, MLA_NOPE + t] = 1.0
            selp[copy * MLA_ROPE + t, :, perm[MLA_NOPE + t]] = 1.0
    wk = jnp.concatenate([k_nope, sel.reshape(LANES, -1)], axis=0).astype(BF16)
    wkp = jnp.concatenate([jnp.zeros_like(k_nope), selp.reshape(LANES, -1)], axis=0).astype(BF16)
    kg = p["mla_k_gain"][i]
    gk = jnp.tile(jnp.pad(kg, (0, MLA_SLOT - MLA_QK)), MLA_HEADS)[None]
    gkp = jnp.tile(jnp.pad(kg[perm] * tail, (0, MLA_SLOT - MLA_QK)), MLA_HEADS)[None]
    wv = w_ukv[:, :, MLA_NOPE:].reshape(MLA_KV_RANK, MLA_HEADS * MLA_V).astype(BF16)

    blk = np.arange(2 * LANES) // NA_HEAD_DIM
    bd = jnp.asarray(blk[:, None] == blk[None, :], BF16)

    return {
        "w_in": w_in,
        "na_qg": jnp.tile(p["na_q_gain"][i], NA_HEADS)[None] * NA_HEAD_DIM ** -0.5,
        "na_kg": jnp.tile(p["na_k_gain"][i], NA_HEADS)[None],
        "bd": bd,
        "ckv_g": p["mla_ckv_gain"][i][None], "cq_g": p["mla_cq_gain"][i][None],
        "ln_g": p["gm_ln_g"][i][None], "ln_b": p["gm_ln_b"][i][None],
        "wq": wq, "wqp": wqp, "gq": gq, "gqp": gqp, "wk": wk, "wkp": wkp, "gk": gk, "gkp": gkp, "wv": wv,
        "w_na_o": p["na_w_o"][i].astype(BF16), "w_gm_o": p["gm_w_o"][i].astype(BF16),
        "w_mla_o": p["mla_w_o"][i].astype(BF16), "w_out": p["w_out"][i].astype(BF16),
        "w1": p["ffn_w1"][i].astype(BF16), "w2": p["ffn_w2"][i].astype(BF16),
        "w_s": p["gm_w_s"][i].astype(BF16),
        "b_s": jnp.repeat(p["gm_b_s"][i].T, GM_GROUP_DIM, axis=1),
    }


def _in_offsets(d):
    off = {"nak": 0}
    off["nav"] = off["nak"] + NA_WIDTH
    off["ckv"] = off["nav"] + NA_WIDTH
    off["kr"] = off["ckv"] + MLA_KV_RANK
    off["naq"] = off["kr"] + MLA_ROPE
    off["gmu"] = off["naq"] + NA_WIDTH
    off["gmv"] = off["gmu"] + GM_WIDTH
    off["cq"] = off["gmv"] + GM_WIDTH
    off["gates"] = off["cq"] + MLA_Q_RANK
    return off


def kernel(x, c, ctx, c_ctx, w_mod, b_mod, g_norm1, g_norm2, w_in, na_q_gain, na_k_gain, na_rpb, na_w_o, gm_ln_g,
           gm_ln_b, gm_w_s, gm_b_s, gm_w_o, mla_cq_gain, mla_ckv_gain, mla_w_uq, mla_w_ukv, mla_q_gain, mla_k_gain,
           mla_w_o, w_out, ffn_w1, ffn_w2):
    batch, seq, d = x.shape
    ctx_len = ctx.shape[1]
    depth = w_mod.shape[0]
    p = dict(w_in=w_in, na_q_gain=na_q_gain, na_k_gain=na_k_gain, na_w_o=na_w_o, gm_ln_g=gm_ln_g, gm_ln_b=gm_ln_b,
             gm_w_s=gm_w_s, gm_b_s=gm_b_s, gm_w_o=gm_w_o, mla_cq_gain=mla_cq_gain, mla_ckv_gain=mla_ckv_gain,
             mla_w_uq=mla_w_uq, mla_w_ukv=mla_w_ukv, mla_q_gain=mla_q_gain, mla_k_gain=mla_k_gain, mla_w_o=mla_w_o,
             w_out=w_out, ffn_w1=ffn_w1, ffn_w2=ffn_w2)
    off = _in_offsets(d)

    mod_rows = 16
    ctx_row = batch
    rows = jnp.concatenate([c, c_ctx[None], jnp.zeros((mod_rows - batch - 1, d), F32)], axis=0)
    mod = _modulation(rows, w_mod, b_mod).reshape(depth, mod_rows, 6, 1, d)

    cos_t, sin_t = _rope_tables(seq)
    cos_c = jnp.ones((ctx_len, LANES), F32)
    sin_c = jnp.zeros((ctx_len, LANES), F32)

    tm = min(512, seq)
    tmc = min(256, ctx_len)
    tiles_per_seq = seq // tm
    tiles_per_ctx = ctx_len // tmc
    lat_row = lambda i: i // tiles_per_seq
    ctx_row_fn = lambda i: ctx_row
    grid_rows = seq // GRID_W

    xs = x.reshape(batch * seq, d)
    cs = ctx.reshape(batch * ctx_len, d)
    for i in range(depth):
        last = i == depth - 1
        lw = _pack_layer(i, p, off)
        mod_l = mod[i]
        gn1 = g_norm1[i][None]
        gn2 = g_norm2[i][None]

        (naq, nak, nav, gmu, gmv, gates, qm, km, vm) = _in_projection(
            xs, mod_l, lat_row, tm, tiles_per_seq, gn1, lw, cos_t, sin_t)
        (naq_c, nak_c, nav_c, gmu_c, gmv_c, gates_c, qm_c, km_c, vm_c) = _in_projection(
            cs, mod_l, ctx_row_fn, tmc, tiles_per_ctx, gn1, lw, cos_c, sin_c)

        bias = _na_bias_table(na_rpb[i], grid_rows)
        y_na = _na_attention(naq, nak, nav, nak_c, nav_c, bias, batch, seq, ctx_len)
        y_mla = _mla_attention(qm, km_c, vm_c, km, vm, batch, seq, ctx_len, seq, tq=min(512, seq), tk=min(512, seq))
        xs_new = _merge_ffn(xs, y_na, y_mla, gmu, gmv, gates, mod_l, lat_row, tm, gn2, lw)

        if not last:
            y_na_c = _na_ctx_attention(naq_c, nak_c, nav_c, batch, ctx_len)
            y_mla_c = _mla_attention(qm_c, km_c, vm_c, None, None, batch, ctx_len, ctx_len, 0, tq=ctx_len, tk=0)
            cs = _merge_ffn(cs, y_na_c, y_mla_c, gmu_c, gmv_c, gates_c, mod_l, ctx_row_fn, tmc, gn2, lw)
        xs = xs_new
    return xs.reshape(batch, seq, d)
```

```python
import functools
import math

import numpy as np
import jax
import jax.numpy as jnp
from jax import lax
from jax.experimental import pallas as pl
from jax.experimental.pallas import tpu as pltpu

F32 = jnp.float32
BF16 = jnp.bfloat16

LANES = 128
VMEM_LIMIT_BYTES = 60 * 1024 * 1024

GRID_W = 64
N_BRANCHES = 3
NA_HEADS = 8
NA_HEAD_DIM = 64
NA_WIDTH = NA_HEADS * NA_HEAD_DIM
NA_WIN_H = 8
NA_WIN_W = 16
GM_GROUPS = 4
GM_CHUNK = 128
GM_GROUP_DIM = 128
GM_WIDTH = GM_GROUPS * GM_GROUP_DIM
MLA_HEADS = 8
MLA_Q_RANK = 256
MLA_KV_RANK = 128
MLA_NOPE = 64
MLA_ROPE = 32
MLA_V = 64
MLA_QK = MLA_NOPE + MLA_ROPE
ROPE_AXIS_PAIRS = MLA_ROPE // 4
ROPE_THETA = 10000.0
NORM_EPS = 1e-6
NEG_INF = -1e30
LOG2E = math.log2(math.e)

NA_QR = 4
NA_BAND = 12
MLA_SLOT = LANES


def _resident(shape):
    zeros = (0,) * len(shape)
    return pl.BlockSpec(shape, lambda *_: zeros, pipeline_mode=pl.Buffered(1))


def _params():
    return pltpu.CompilerParams(vmem_limit_bytes=VMEM_LIMIT_BYTES)


def _dot(a, b):
    return jnp.dot(a, b, preferred_element_type=F32)


def _dot_t(a, b):
    return lax.dot_general(a, b, (((1,), (1,)), ((), ())), preferred_element_type=F32)


def _mod_kernel(s_ref, w_ref, b_ref, o_ref):
    s = s_ref[...]
    s = s * jax.nn.sigmoid(s)
    o_ref[...] = _dot(s.astype(BF16), w_ref[...].astype(BF16)) + b_ref[...]


def _modulation(rows, w_mod, b_mod):
    depth, d, n = w_mod.shape
    r = rows.shape[0]
    tn = 1536
    return pl.pallas_call(
        _mod_kernel,
        grid=(depth, n // tn),
        in_specs=[
            pl.BlockSpec((r, d), lambda l, j: (0, 0)),
            pl.BlockSpec((None, d, tn), lambda l, j: (l, 0, j)),
            pl.BlockSpec((None, 1, tn), lambda l, j: (l, 0, j)),
        ],
        out_specs=pl.BlockSpec((None, r, tn), lambda l, j: (l, 0, j)),
        out_shape=jax.ShapeDtypeStruct((depth, r, n), F32),
        compiler_params=_params(),
        name="modulation",
    )(rows, w_mod, b_mod.reshape(depth, 1, n))


C_NAQ = 0
C_NAK = C_NAQ + NA_WIDTH
C_CKV = C_NAK + NA_WIDTH
C_KR = C_CKV + MLA_KV_RANK
C_CQ = C_KR + LANES
C_GMU = C_CQ + MLA_Q_RANK
C_GMV = C_GMU + GM_WIDTH
C_GATES = C_GMV + GM_WIDTH
C_TOTAL = C_GATES + N_BRANCHES * 1024


def _rms(t, width):
    return t * lax.rsqrt(jnp.sum(t * t, axis=-1, keepdims=True) * (1.0 / width) + NORM_EPS)


def _gelu(t):
    return 0.5 * t * (1.0 + lax.erf(t * math.sqrt(0.5)))


def _head64_norm(p, bd):
    p2 = (p * p).astype(BF16)
    half = 2 * LANES
    ss = jnp.concatenate([_dot(p2[:, :half], bd), _dot(p2[:, half:], bd)], axis=1)
    return p * lax.rsqrt(ss * (1.0 / NA_HEAD_DIM) + NORM_EPS)


def _inproj_kernel(x_ref, sh_ref, sc_ref, gn_ref, w_ref, wnavt_ref, qg_ref, kg_ref, bd_ref, ckvg_ref, cqg_ref, lng_ref,
                   lnb_ref, wq_ref, wqp_ref, gq_ref, gqp_ref, wk_ref, wkp_ref, gk_ref, gkp_ref, wvt_ref, cos_ref,
                   sin_ref, naq_ref, nak_ref, navt_ref, gmu_ref, gmv_ref, gates_ref, qm_ref, km_ref, vmt_ref):
    x = x_ref[...]
    d = x.shape[-1]
    h = _rms(x, d) * gn_ref[...]
    h = h * (1.0 + sc_ref[...]) + sh_ref[...]
    hb = h.astype(BF16)

    def proj(c0, width):
        return _dot(hb, w_ref[:, c0:c0 + width])

    bd = bd_ref[...]
    naq_ref[...] = (_head64_norm(proj(C_NAQ, NA_WIDTH), bd) * qg_ref[...]).astype(BF16)
    nak_ref[...] = (_head64_norm(proj(C_NAK, NA_WIDTH), bd) * kg_ref[...]).astype(BF16)
    navt_ref[...] = _dot_t(wnavt_ref[...], hb).astype(BF16)

    gmu_ref[...] = _gelu(proj(C_GMU, GM_WIDTH)).astype(BF16)
    gv = _gelu(proj(C_GMV, GM_WIDTH))
    mu = jnp.mean(gv, axis=-1, keepdims=True)
    gc = gv - mu
    var = jnp.mean(gc * gc, axis=-1, keepdims=True)
    gmv_ref[...] = (gc * lax.rsqrt(var + NORM_EPS) * lng_ref[...] + lnb_ref[...]).astype(BF16)

    n_gate = gates_ref.shape[-1]
    gate_chunk = 512
    for j in range(n_gate // gate_chunk):
        cs = slice(j * gate_chunk, (j + 1) * gate_chunk)
        gates_ref[:, cs] = jax.nn.sigmoid(proj(C_GATES + j * gate_chunk, gate_chunk)).astype(BF16)

    ckv_n = (_rms(proj(C_CKV, MLA_KV_RANK), MLA_KV_RANK) * ckvg_ref[...]).astype(BF16)
    cq_n = (_rms(proj(C_CQ, MLA_Q_RANK), MLA_Q_RANK) * cqg_ref[...]).astype(BF16)
    kr = proj(C_KR, LANES)
    kr_hi = kr.astype(BF16)
    kr_lo = (kr - kr_hi.astype(F32)).astype(BF16)
    lane = lax.broadcasted_iota(jnp.int32, (1, LANES), 1)
    kr_hl = jnp.where(lane < MLA_ROPE, kr_hi, kr_lo)
    lhs_k = jnp.concatenate([ckv_n, kr_hl], axis=1)

    vmt_ref[...] = _dot_t(wvt_ref[...], ckv_n).astype(BF16)

    cosv = cos_ref[...]
    sinv = sin_ref[...]

    def heads(lhs, w_r, wp_r, g_r, gp_r, out_r):
        xa = _dot(lhs, w_r[...])
        xpa = _dot(lhs, wp_r[...])
        for hd in range(MLA_HEADS):
            sl = slice(hd * MLA_SLOT, (hd + 1) * MLA_SLOT)
            xh = xa[:, sl]
            r = lax.rsqrt(jnp.sum(xh * xh, axis=-1, keepdims=True) * (1.0 / MLA_QK) + NORM_EPS)
            out_r[:, sl] = (r * (xh * (g_r[:, sl] * cosv) + xpa[:, sl] * (gp_r[:, sl] * sinv))).astype(BF16)

    heads(cq_n, wq_ref, wqp_ref, gq_ref, gqp_ref, qm_ref)
    heads(lhs_k, wk_ref, wkp_ref, gk_ref, gkp_ref, km_ref)


def _in_projection(xs, mod_l, mod_row_fn, tm, tiles_per_seq, gn, lw, cos_t, sin_t):
    r, d = xs.shape
    n_tiles = r // tm
    row = lambda w: pl.BlockSpec((tm, w), lambda i: (i, 0))
    modspec = lambda chunk: pl.BlockSpec((None, None, 1, d), lambda i: (mod_row_fn(i), chunk, 0, 0))
    pos = pl.BlockSpec((tm, LANES), lambda i: (i % tiles_per_seq, 0))
    in_specs = [
        row(d), modspec(0), modspec(1), _resident((1, d)), _resident((d, C_TOTAL)), _resident((NA_WIDTH, d)),
        _resident((1, NA_WIDTH)), _resident((1, NA_WIDTH)), _resident((2 * LANES, 2 * LANES)),
        _resident((1, MLA_KV_RANK)), _resident((1, MLA_Q_RANK)), _resident((1, GM_WIDTH)), _resident((1, GM_WIDTH)),
        _resident((MLA_Q_RANK, MLA_HEADS * MLA_SLOT)), _resident((MLA_Q_RANK, MLA_HEADS * MLA_SLOT)),
        _resident((1, MLA_HEADS * MLA_SLOT)), _resident((1, MLA_HEADS * MLA_SLOT)),
        _resident((2 * LANES, MLA_HEADS * MLA_SLOT)), _resident((2 * LANES, MLA_HEADS * MLA_SLOT)),
        _resident((1, MLA_HEADS * MLA_SLOT)), _resident((1, MLA_HEADS * MLA_SLOT)),
        _resident((MLA_HEADS * MLA_V, MLA_KV_RANK)), pos, pos,
    ]
    outs = [("naq", NA_WIDTH, False), ("nak", NA_WIDTH, False), ("navt", NA_WIDTH, True), ("gmu", GM_WIDTH, False),
            ("gmv", GM_WIDTH, False), ("gates", N_BRANCHES * d, False), ("qm", MLA_HEADS * MLA_SLOT, False),
            ("km", MLA_HEADS * MLA_SLOT, False), ("vmt", MLA_HEADS * MLA_V, True)]
    col = lambda w: pl.BlockSpec((w, tm), lambda i: (0, i))
    return pl.pallas_call(
        _inproj_kernel,
        grid=(n_tiles,),
        in_specs=in_specs,
        out_specs=[col(w) if t else row(w) for _, w, t in outs],
        out_shape=[jax.ShapeDtypeStruct((w, r) if t else (r, w), BF16) for _, w, t in outs],
        compiler_params=_params(),
        name="in_projection",
    )(xs, mod_l, mod_l, gn, lw["w_in"], lw["w_navt"], lw["na_qg"], lw["na_kg"], lw["bd"], lw["ckv_g"], lw["cq_g"],
      lw["ln_g"],
      lw["ln_b"], lw["wq"], lw["wqp"], lw["gq"], lw["gqp"], lw["wk"], lw["wkp"], lw["gk"], lw["gkp"], lw["wvt"],
      cos_t, sin_t)


def _pair_softmax_pv(q_pair, blocks):
    lane = lax.broadcasted_iota(jnp.int32, (1, LANES), 1)
    row = lax.broadcasted_iota(jnp.int32, (LANES, 1), 0)
    outs = []
    for hh in range(2):
        own_lane = (lane < NA_HEAD_DIM) if hh == 0 else (lane >= NA_HEAD_DIM)
        own_row = (row < NA_HEAD_DIM) if hh == 0 else (row >= NA_HEAD_DIM)
        qh = jnp.where(own_lane, q_pair, jnp.zeros_like(q_pair))
        sts = []
        for k_pair, _, bias_t in blocks:
            st = _dot_t(k_pair, qh)
            sts.append(st if bias_t is None else st + bias_t[hh])
        m = None
        for st in sts:
            mb = jnp.max(st, axis=0, keepdims=True)
            m = mb if m is None else jnp.maximum(m, mb)
        acc = None
        for st, (_, vt_pair, _) in zip(sts, blocks):
            pt = jnp.exp2(st - m).astype(BF16)
            part = _dot(jnp.where(own_row, vt_pair, jnp.ones_like(vt_pair)), pt)
            acc = part if acc is None else acc + part
        lo, hi = acc[:NA_HEAD_DIM], acc[NA_HEAD_DIM:]
        outs.append(lo / hi if hh == 0 else hi / lo)
    return jnp.concatenate(outs, axis=0).T


def _na_kernel(q_ref, k_ref, vt_ref, kc_ref, vct_ref, bias_ref, o_ref, *, grid_rows):
    rb = pl.program_id(1)
    band0 = jnp.clip(NA_QR * rb - NA_WIN_H // 2, 0, grid_rows - NA_BAND)
    k0 = pl.multiple_of(band0 * GRID_W, 2 * LANES)
    nk = NA_BAND * GRID_W
    for hp in range(NA_HEADS // 2):
        cs = slice(hp * LANES, (hp + 1) * LANES)
        blocks = [
            (k_ref[pl.ds(k0, nk), cs], vt_ref[cs, pl.ds(k0, nk)], (bias_ref[0, 2 * hp], bias_ref[0, 2 * hp + 1])),
            (kc_ref[:, cs], vct_ref[cs, :], None),
        ]
        o_ref[:, cs] = _pair_softmax_pv(q_ref[:, cs], blocks).astype(BF16)


def _na_attention(q, k, vt, kc, vct, bias, batch, seq, ctx_len):
    grid_rows = seq // GRID_W
    n_rb = grid_rows // NA_QR
    tq = NA_QR * GRID_W
    nk = NA_BAND * GRID_W

    def bias_cls(b, rb):
        return (jnp.where(rb == 0, 0, jnp.where(rb == n_rb - 1, 2, 1)), 0, 0, 0)

    return pl.pallas_call(
        functools.partial(_na_kernel, grid_rows=grid_rows),
        grid=(batch, n_rb),
        in_specs=[
            pl.BlockSpec((tq, NA_WIDTH), lambda b, rb: (b * n_rb + rb, 0)),
            pl.BlockSpec((seq, NA_WIDTH), lambda b, rb: (b, 0)),
            pl.BlockSpec((NA_WIDTH, seq), lambda b, rb: (0, b)),
            pl.BlockSpec((ctx_len, NA_WIDTH), lambda b, rb: (b, 0)),
            pl.BlockSpec((NA_WIDTH, ctx_len), lambda b, rb: (0, b)),
            pl.BlockSpec((1, NA_HEADS, nk, tq), bias_cls),
        ],
        out_specs=pl.BlockSpec((tq, NA_WIDTH), lambda b, rb: (b * n_rb + rb, 0)),
        out_shape=jax.ShapeDtypeStruct((batch * seq, NA_WIDTH), BF16),
        compiler_params=_params(),
        name="na_attention",
    )(q, k, vt, kc, vct, bias)


def _na_bias_table(rpb, grid_rows):
    n_rb = grid_rows // NA_QR
    kh, kw = NA_WIN_H, NA_WIN_W
    cq = np.arange(GRID_W)
    c0 = np.clip(cq - kw // 2, 0, GRID_W - kw)
    col_in = (cq[None, :] >= c0[:, None]) & (cq[None, :] < c0[:, None] + kw)
    dc = np.clip(cq[None, :] - cq[:, None], -(kw - 1), kw - 1) + kw - 1
    dr = np.zeros((3, NA_QR, NA_BAND), np.int32)
    ok = np.zeros((3, NA_QR, NA_BAND), bool)
    for c, rb in enumerate((0, 1, n_rb - 1)):
        band0 = int(np.clip(NA_QR * rb - kh // 2, 0, grid_rows - NA_BAND))
        for i in range(NA_QR):
            rq = NA_QR * rb + i
            r0 = int(np.clip(rq - kh // 2, 0, grid_rows - kh))
            for j in range(NA_BAND):
                kr = band0 + j
                ok[c, i, j] = r0 <= kr < r0 + kh
                dr[c, i, j] = int(np.clip(kr - rq + kh - 1, 0, 2 * kh - 2))
    t = rpb[:, dr.reshape(-1), :][:, :, dc]
    t = t.reshape(NA_HEADS, 3, NA_QR, NA_BAND, GRID_W, GRID_W)
    mask = ok[None, :, :, :, None, None] & col_in[None, None, None, None, :, :]
    t = jnp.where(mask, t * LOG2E, NEG_INF)
    t = t.transpose(1, 0, 3, 5, 2, 4)
    return t.reshape(3, NA_HEADS, NA_BAND * GRID_W, NA_QR * GRID_W).astype(F32)


def _mla_kernel(q_ref, kc_ref, vct_ref, *rest, n_kb, tk):
    if n_kb:
        k_ref, vt_ref, o_ref = rest
    else:
        (o_ref,) = rest
    tq = q_ref.shape[0]
    row = lax.broadcasted_iota(jnp.int32, (2 * MLA_V, 1), 0)
    lo = row < MLA_V
    heads = (slice(0, MLA_SLOT), slice(MLA_SLOT, 2 * MLA_SLOT))
    q = tuple(q_ref[:, hs] for hs in heads)

    def scores(k_blk):
        return tuple(_dot_t(k_blk[:, heads[h]], q[h]) for h in range(2))

    def values(vt_blk):
        one = jnp.ones_like(vt_blk)
        return jnp.where(lo, vt_blk, one), jnp.where(lo, one, vt_blk)

    def update(st, m, acc, vt_aug):
        m_new = jnp.maximum(m, jnp.max(st, axis=0, keepdims=True))
        pt = jnp.exp2(st - m_new).astype(BF16)
        return m_new, jnp.exp2(m - m_new) * acc + _dot(vt_aug, pt)

    def process(st, state, vt_blk):
        vt_aug = values(vt_blk)
        out = ()
        for h in range(2):
            out += update(st[h], state[2 * h], state[2 * h + 1], vt_aug[h])
        return out

    m0 = jnp.full((1, tq), NEG_INF, F32)
    a0 = jnp.zeros((2 * MLA_V, tq), F32)
    state = (m0, a0, m0, a0)
    s_cur = scores(kc_ref[...])
    if n_kb:
        s_nxt = scores(k_ref[pl.ds(0, tk), :])
        state = process(s_cur, state, vct_ref[...])

        for j in range(n_kb - 1):
            s_cur, s_nxt = s_nxt, scores(k_ref[pl.ds((j + 1) * tk, tk), :])
            state = process(s_cur, state, vt_ref[:, pl.ds(j * tk, tk)])
        state = process(s_nxt, state, vt_ref[:, pl.ds((n_kb - 1) * tk, tk)])
    else:
        state = process(s_cur, state, vct_ref[...])
    o0 = state[1][:MLA_V] / state[1][MLA_V:]
    o1 = state[3][MLA_V:] / state[3][:MLA_V]
    o_ref[...] = jnp.concatenate([o0, o1], axis=0).T.astype(BF16)


def _mla_attention(q, kc, vct, k, vt, batch, q_len, ctx_len, kv_len, tq, tk):
    n_q = q_len // tq
    n_pairs = MLA_HEADS // 2
    n_kb = kv_len // tk if k is not None else 0
    in_specs = [
        pl.BlockSpec((tq, 2 * MLA_SLOT), lambda b, p, i: (b * n_q + i, p)),
        pl.BlockSpec((ctx_len, 2 * MLA_SLOT), lambda b, p, i: (b, p)),
        pl.BlockSpec((2 * MLA_V, ctx_len), lambda b, p, i: (p, b)),
    ]
    args = [q, kc, vct]
    if n_kb:
        in_specs += [
            pl.BlockSpec((kv_len, 2 * MLA_SLOT), lambda b, p, i: (b, p)),
            pl.BlockSpec((2 * MLA_V, kv_len), lambda b, p, i: (p, b)),
        ]
        args += [k, vt]
    return pl.pallas_call(
        functools.partial(_mla_kernel, n_kb=n_kb, tk=tk),
        grid=(batch, n_pairs, n_q),
        in_specs=in_specs,
        out_specs=pl.BlockSpec((tq, LANES), lambda b, p, i: (b * n_q + i, p)),
        out_shape=jax.ShapeDtypeStruct((batch * q_len, MLA_HEADS * MLA_V), BF16),
        compiler_params=_params(),
        name="mla_attention" if n_kb else "mla_ctx_attention",
    )(*args)


def _na_ctx_kernel(q_ref, k_ref, vt_ref, o_ref):
    for hp in range(NA_HEADS // 2):
        cs = slice(hp * LANES, (hp + 1) * LANES)
        o_ref[:, cs] = _pair_softmax_pv(q_ref[:, cs], [(k_ref[:, cs], vt_ref[cs, :], None)]).astype(BF16)


def _na_ctx_attention(q, k, vt, batch, ctx_len):
    spec = pl.BlockSpec((ctx_len, NA_WIDTH), lambda b: (b, 0))
    return pl.pallas_call(
        _na_ctx_kernel,
        grid=(batch,),
        in_specs=[spec, spec, pl.BlockSpec((NA_WIDTH, ctx_len), lambda b: (0, b))],
        out_specs=spec,
        out_shape=jax.ShapeDtypeStruct((batch * ctx_len, NA_WIDTH), BF16),
        compiler_params=_params(),
        name="na_ctx_attention",
    )(q, k, vt)


FFN_HIDDEN_CHUNK = 512


def _merge_ffn_kernel(x_ref, yna_ref, ymla_ref, gmu_ref, gmv_ref, gates_ref, gt1_ref, sh2_ref, sc2_ref, gt2_ref,
                      gn2_ref, wna_ref, wgm_ref, wmla_ref, wout_ref, w1_ref, w2_ref, ws_ref, bsb_ref, o_ref,
                      ygm_ref):
    tm, d = x_ref.shape
    for c in range(tm // GM_CHUNK):
        rs = slice(c * GM_CHUNK, (c + 1) * GM_CHUNK)
        for g in range(GM_GROUPS):
            cs = slice(g * GM_GROUP_DIM, (g + 1) * GM_GROUP_DIM)
            mixed = _dot(ws_ref[g], gmv_ref[rs, cs]) + bsb_ref[:, cs]
            ygm_ref[rs, cs] = (gmu_ref[rs, cs].astype(F32) * mixed).astype(BF16)

    y = gates_ref[:, 0:d].astype(F32) * _dot(yna_ref[...], wna_ref[...])
    y = y + gates_ref[:, d:2 * d].astype(F32) * _dot(ygm_ref[...], wgm_ref[...])
    y = y + gates_ref[:, 2 * d:3 * d].astype(F32) * _dot(ymla_ref[...], wmla_ref[...])
    xn = x_ref[...] + gt1_ref[...] * _dot(y.astype(BF16), wout_ref[...])

    h2 = _rms(xn, d) * gn2_ref[...]
    h2 = (h2 * (1.0 + sc2_ref[...]) + sh2_ref[...]).astype(BF16)
    hidden = w1_ref.shape[1]
    acc = None
    for c in range(hidden // FFN_HIDDEN_CHUNK):
        cs = slice(c * FFN_HIDDEN_CHUNK, (c + 1) * FFN_HIDDEN_CHUNK)
        a = jnp.maximum(_dot(h2, w1_ref[:, cs]), 0.0)
        part = _dot((a * a).astype(BF16), w2_ref[cs, :])
        acc = part if acc is None else acc + part
    o_ref[...] = xn + gt2_ref[...] * acc


def _merge_ffn(xs, y_na, y_mla, gm_u, gm_v, gates, mod_l, mod_row_fn, tm, gn2, lw):
    r, d = xs.shape
    hidden = lw["w1"].shape[1]
    row = lambda w: pl.BlockSpec((tm, w), lambda i: (i, 0))
    modspec = lambda chunk: pl.BlockSpec((None, None, 1, d), lambda i: (mod_row_fn(i), chunk, 0, 0))
    in_specs = [
        row(d), row(NA_WIDTH), row(MLA_HEADS * MLA_V), row(GM_WIDTH), row(GM_WIDTH), row(N_BRANCHES * d),
        modspec(2), modspec(3), modspec(4), modspec(5), _resident((1, d)),
        _resident((NA_WIDTH, d)), _resident((GM_WIDTH, d)), _resident((MLA_HEADS * MLA_V, d)), _resident((d, d)),
        _resident((d, hidden)), _resident((hidden, d)), _resident((GM_GROUPS, GM_CHUNK, GM_CHUNK)),
        _resident((GM_CHUNK, GM_WIDTH)),
    ]
    return pl.pallas_call(
        _merge_ffn_kernel,
        grid=(r // tm,),
        in_specs=in_specs,
        out_specs=row(d),
        out_shape=jax.ShapeDtypeStruct((r, d), F32),
        scratch_shapes=[pltpu.VMEM((tm, GM_WIDTH), BF16)],
        compiler_params=_params(),
        name="merge_ffn",
    )(xs, y_na, y_mla, gm_u, gm_v, gates, mod_l, mod_l, mod_l, mod_l, gn2, lw["w_na_o"], lw["w_gm_o"],
      lw["w_mla_o"], lw["w_out"], lw["w1"], lw["w2"], lw["w_s"], lw["b_s"])


def _rope_partner():
    perm = np.arange(MLA_QK)
    for u in range(MLA_ROPE):
        seg, half, pair = u // (2 * ROPE_AXIS_PAIRS), (u // ROPE_AXIS_PAIRS) % 2, u % ROPE_AXIS_PAIRS
        perm[MLA_NOPE + u] = MLA_NOPE + seg * 2 * ROPE_AXIS_PAIRS + (1 - half) * ROPE_AXIS_PAIRS + pair
    tail = np.arange(MLA_QK) >= MLA_NOPE
    return perm, tail


def _pad_heads(t):
    pad = [(0, 0)] * (t.ndim - 1) + [(0, MLA_SLOT - MLA_QK)]
    t = jnp.pad(t, pad)
    return t.reshape(t.shape[:-2] + (MLA_HEADS * MLA_SLOT,))


def _rope_tables(seq):
    t = jnp.arange(seq)
    rows = (t // GRID_W).astype(F32)
    colsv = (t % GRID_W).astype(F32)
    freqs = ROPE_THETA ** (-jnp.arange(ROPE_AXIS_PAIRS, dtype=F32) / ROPE_AXIS_PAIRS)
    ang = jnp.stack([rows[:, None] * freqs, colsv[:, None] * freqs], axis=1)
    cos, sin = jnp.cos(ang), jnp.sin(ang)
    cos_t = jnp.stack([cos, cos], axis=2).reshape(seq, MLA_ROPE)
    sin_t = jnp.stack([-sin, sin], axis=2).reshape(seq, MLA_ROPE)
    ones = jnp.ones((seq, MLA_NOPE), F32)
    padc = jnp.ones((seq, MLA_SLOT - MLA_QK), F32)
    cos_full = jnp.concatenate([ones, cos_t, padc], axis=1)
    sin_full = jnp.concatenate([0 * ones, sin_t, 0 * padc], axis=1)
    return cos_full, sin_full


def _pack_layer(i, p, off):
    d = p["w_in"].shape[1]
    w = p["w_in"][i]
    kr = w[:, off["kr"]:off["kr"] + MLA_ROPE]
    w_in = jnp.concatenate([
        w[:, off["naq"]:off["naq"] + NA_WIDTH], w[:, off["nak"]:off["nak"] + NA_WIDTH],
        w[:, off["ckv"]:off["ckv"] + MLA_KV_RANK],
        kr, kr, jnp.zeros((d, LANES - 2 * MLA_ROPE), w.dtype),
        w[:, off["cq"]:off["cq"] + MLA_Q_RANK], w[:, off["gmu"]:off["gmu"] + GM_WIDTH],
        w[:, off["gmv"]:off["gmv"] + GM_WIDTH], w[:, off["gates"]:off["gates"] + N_BRANCHES * d],
    ], axis=1).astype(BF16)

    perm, tail = _rope_partner()
    q_scale = MLA_QK ** -0.5 * LOG2E
    w_uq = p["mla_w_uq"][i].reshape(MLA_Q_RANK, MLA_HEADS, MLA_QK)
    wq = _pad_heads(w_uq).astype(BF16)
    wqp = _pad_heads(w_uq[:, :, perm] * tail).astype(BF16)
    qg = p["mla_q_gain"][i]
    gq = jnp.tile(jnp.pad(qg, (0, MLA_SLOT - MLA_QK)), MLA_HEADS)[None] * q_scale
    gqp = jnp.tile(jnp.pad(qg[perm] * tail, (0, MLA_SLOT - MLA_QK)), MLA_HEADS)[None] * q_scale

    w_ukv = p["mla_w_ukv"][i].reshape(MLA_KV_RANK, MLA_HEADS, MLA_NOPE + MLA_V)
    k_nope = jnp.pad(w_ukv[:, :, :MLA_NOPE], ((0, 0), (0, 0), (0, MLA_SLOT - MLA_NOPE)))
    k_nope = k_nope.reshape(MLA_KV_RANK, MLA_HEADS * MLA_SLOT)
    sel = np.zeros((LANES, MLA_HEADS, MLA_SLOT), np.float32)
    selp = np.zeros((LANES, MLA_HEADS, MLA_SLOT), np.float32)
    for t in range(MLA_ROPE):
        for copy in range(2):
            sel[copy * MLA_ROPE + t, :, MLA_NOPE + t] = 1.0
            selp[copy * MLA_ROPE + t, :, perm[MLA_NOPE + t]] = 1.0
    wk = jnp.concatenate([k_nope, sel.reshape(LANES, -1)], axis=0).astype(BF16)
    wkp = jnp.concatenate([jnp.zeros_like(k_nope), selp.reshape(LANES, -1)], axis=0).astype(BF16)
    kg = p["mla_k_gain"][i]
    gk = jnp.tile(jnp.pad(kg, (0, MLA_SLOT - MLA_QK)), MLA_HEADS)[None]
    gkp = jnp.tile(jnp.pad(kg[perm] * tail, (0, MLA_SLOT - MLA_QK)), MLA_HEADS)[None]
    wvt = w_ukv[:, :, MLA_NOPE:].reshape(MLA_KV_RANK, MLA_HEADS * MLA_V).T.astype(BF16)

    blk = np.arange(2 * LANES) // NA_HEAD_DIM
    bd = jnp.asarray(blk[:, None] == blk[None, :], BF16)

    return {
        "w_in": w_in,
        "w_navt": w[:, off["nav"]:off["nav"] + NA_WIDTH].T.astype(BF16),
        "na_qg": jnp.tile(p["na_q_gain"][i], NA_HEADS)[None] * (NA_HEAD_DIM ** -0.5 * LOG2E),
        "na_kg": jnp.tile(p["na_k_gain"][i], NA_HEADS)[None],
        "bd": bd,
        "ckv_g": p["mla_ckv_gain"][i][None], "cq_g": p["mla_cq_gain"][i][None],
        "ln_g": p["gm_ln_g"][i][None], "ln_b": p["gm_ln_b"][i][None],
        "wq": wq, "wqp": wqp, "gq": gq, "gqp": gqp, "wk": wk, "wkp": wkp, "gk": gk, "gkp": gkp, "wvt": wvt,
        "w_na_o": p["na_w_o"][i].astype(BF16), "w_gm_o": p["gm_w_o"][i].astype(BF16),
        "w_mla_o": p["mla_w_o"][i].astype(BF16), "w_out": p["w_out"][i].astype(BF16),
        "w1": p["ffn_w1"][i].astype(BF16), "w2": p["ffn_w2"][i].astype(BF16),
        "w_s": p["gm_w_s"][i].astype(BF16),
        "b_s": jnp.repeat(p["gm_b_s"][i].T, GM_GROUP_DIM, axis=1),
    }


def _in_offsets(d):
    off = {"nak": 0}
    off["nav"] = off["nak"] + NA_WIDTH
    off["ckv"] = off["nav"] + NA_WIDTH
    off["kr"] = off["ckv"] + MLA_KV_RANK
    off["naq"] = off["kr"] + MLA_ROPE
    off["gmu"] = off["naq"] + NA_WIDTH
    off["gmv"] = off["gmu"] + GM_WIDTH
    off["cq"] = off["gmv"] + GM_WIDTH
    off["gates"] = off["cq"] + MLA_Q_RANK
    return off


def kernel(x, c, ctx, c_ctx, w_mod, b_mod, g_norm1, g_norm2, w_in, na_q_gain, na_k_gain, na_rpb, na_w_o, gm_ln_g,
           gm_ln_b, gm_w_s, gm_b_s, gm_w_o, mla_cq_gain, mla_ckv_gain, mla_w_uq, mla_w_ukv, mla_q_gain, mla_k_gain,
           mla_w_o, w_out, ffn_w1, ffn_w2):
    batch, seq, d = x.shape
    ctx_len = ctx.shape[1]
    depth = w_mod.shape[0]
    p = dict(w_in=w_in, na_q_gain=na_q_gain, na_k_gain=na_k_gain, na_w_o=na_w_o, gm_ln_g=gm_ln_g, gm_ln_b=gm_ln_b,
             gm_w_s=gm_w_s, gm_b_s=gm_b_s, gm_w_o=gm_w_o, mla_cq_gain=mla_cq_gain, mla_ckv_gain=mla_ckv_gain,
             mla_w_uq=mla_w_uq, mla_w_ukv=mla_w_ukv, mla_q_gain=mla_q_gain, mla_k_gain=mla_k_gain, mla_w_o=mla_w_o,
             w_out=w_out, ffn_w1=ffn_w1, ffn_w2=ffn_w2)
    off = _in_offsets(d)

    mod_rows = 16
    ctx_row = batch
    rows = jnp.concatenate([c, c_ctx[None], jnp.zeros((mod_rows - batch - 1, d), F32)], axis=0)
    mod = _modulation(rows, w_mod, b_mod).reshape(depth, mod_rows, 6, 1, d)

    cos_t, sin_t = _rope_tables(seq)
    cos_c = jnp.ones((ctx_len, LANES), F32)
    sin_c = jnp.zeros((ctx_len, LANES), F32)

    tm = min(512, seq)
    tmc = min(256, ctx_len)
    tiles_per_seq = seq // tm
    tiles_per_ctx = ctx_len // tmc
    lat_row = lambda i: i // tiles_per_seq
    ctx_row_fn = lambda i: ctx_row
    grid_rows = seq // GRID_W

    xs = x.reshape(batch * seq, d)
    cs = ctx.reshape(batch * ctx_len, d)
    for i in range(depth):
        last = i == depth - 1
        lw = _pack_layer(i, p, off)
        mod_l = mod[i]
        gn1 = g_norm1[i][None]
        gn2 = g_norm2[i][None]

        (naq, nak, nav, gmu, gmv, gates, qm, km, vm) = _in_projection(
            xs, mod_l, lat_row, tm, tiles_per_seq, gn1, lw, cos_t, sin_t)
        (naq_c, nak_c, nav_c, gmu_c, gmv_c, gates_c, qm_c, km_c, vm_c) = _in_projection(
            cs, mod_l, ctx_row_fn, tmc, tiles_per_ctx, gn1, lw, cos_c, sin_c)

        bias = _na_bias_table(na_rpb[i], grid_rows)
        y_na = _na_attention(naq, nak, nav, nak_c, nav_c, bias, batch, seq, ctx_len)
        y_mla = _mla_attention(qm, km_c, vm_c, km, vm, batch, seq, ctx_len, seq, tq=min(1024, seq), tk=min(512, seq))
        xs_new = _merge_ffn(xs, y_na, y_mla, gmu, gmv, gates, mod_l, lat_row, tm, gn2, lw)

        if not last:
            y_na_c = _na_ctx_attention(naq_c, nak_c, nav_c, batch, ctx_len)
            y_mla_c = _mla_attention(qm_c, km_c, vm_c, None, None, batch, ctx_len, ctx_len, 0, tq=ctx_len, tk=0)
            cs = _merge_ffn(cs, y_na_c, y_mla_c, gmu_c, gmv_c, gates_c, mod_l, ctx_row_fn, tmc, gn2, lw)
        xs = xs_new
    return xs.reshape(batch, seq, d)
```

```python
import functools
import math

import numpy as np
import jax
import jax.numpy as jnp
from jax import lax
from jax.experimental import pallas as pl
from jax.experimental.pallas import tpu as pltpu

F32 = jnp.float32
BF16 = jnp.bfloat16

LANES = 128
BF16_SUBLANES = 16
VMEM_LIMIT_BYTES = 60 * 1024 * 1024

GRID_W = 64
N_BRANCHES = 3
NA_HEADS = 8
NA_HEAD_DIM = 64
NA_WIDTH = NA_HEADS * NA_HEAD_DIM
NA_WIN_H = 8
NA_WIN_W = 16
GM_GROUPS = 4
GM_CHUNK = 128
GM_GROUP_DIM = 128
GM_WIDTH = GM_GROUPS * GM_GROUP_DIM
MLA_HEADS = 8
MLA_Q_RANK = 256
MLA_KV_RANK = 128
MLA_NOPE = 64
MLA_ROPE = 32
MLA_V = 64
MLA_QK = MLA_NOPE + MLA_ROPE
ROPE_AXIS_PAIRS = MLA_ROPE // 4
ROPE_THETA = 10000.0
NORM_EPS = 1e-6
NEG_INF = -1e30
LOG2E = math.log2(math.e)

NA_QR = 4
NA_BAND = 12
MLA_SLOT = LANES


def _resident(shape):
    zeros = (0,) * len(shape)
    return pl.BlockSpec(shape, lambda *_: zeros, pipeline_mode=pl.Buffered(1))


def _params():
    return pltpu.CompilerParams(vmem_limit_bytes=VMEM_LIMIT_BYTES)


def _dot(a, b):
    return jnp.dot(a, b, preferred_element_type=F32)


def _dot_t(a, b):
    return lax.dot_general(a, b, (((1,), (1,)), ((), ())), preferred_element_type=F32)


def _mod_kernel(s_ref, w_ref, b_ref, o_ref):
    s = s_ref[...]
    s = s * jax.nn.sigmoid(s)
    o_ref[...] = _dot(s.astype(BF16), w_ref[...].astype(BF16)) + b_ref[...]


def _modulation(rows, w_mod, b_mod):
    depth, d, n = w_mod.shape
    r = rows.shape[0]
    tn = 1536
    return pl.pallas_call(
        _mod_kernel,
        grid=(depth, n // tn),
        in_specs=[
            pl.BlockSpec((r, d), lambda l, j: (0, 0)),
            pl.BlockSpec((None, d, tn), lambda l, j: (l, 0, j)),
            pl.BlockSpec((None, 1, tn), lambda l, j: (l, 0, j)),
        ],
        out_specs=pl.BlockSpec((None, r, tn), lambda l, j: (l, 0, j)),
        out_shape=jax.ShapeDtypeStruct((depth, r, n), F32),
        compiler_params=_params(),
        name="modulation",
    )(rows, w_mod, b_mod.reshape(depth, 1, n))


C_NAQ = 0
C_NAK = C_NAQ + NA_WIDTH
C_CKV = C_NAK + NA_WIDTH
C_KR = C_CKV + MLA_KV_RANK
C_CQ = C_KR + LANES
C_GMU = C_CQ + MLA_Q_RANK
C_GMV = C_GMU + GM_WIDTH
C_GATES = C_GMV + GM_WIDTH
C_TOTAL = C_GATES + N_BRANCHES * 1024


def _rms(t, width):
    return t * lax.rsqrt(jnp.sum(t * t, axis=-1, keepdims=True) * (1.0 / width) + NORM_EPS)


def _gelu(t):
    return 0.5 * t * (1.0 + lax.erf(t * math.sqrt(0.5)))


def _head64_norm(p, bd):
    p2 = (p * p).astype(BF16)
    half = 2 * LANES
    ss = jnp.concatenate([_dot(p2[:, :half], bd), _dot(p2[:, half:], bd)], axis=1)
    return p * lax.rsqrt(ss * (1.0 / NA_HEAD_DIM) + NORM_EPS)


def _inproj_kernel(x_ref, sh_ref, sc_ref, gn_ref, w_ref, wnavt_ref, qg_ref, kg_ref, bd_ref, ckvg_ref, cqg_ref, lng_ref,
                   lnb_ref, wq_ref, wqp_ref, gq_ref, gqp_ref, wk_ref, wkp_ref, gk_ref, gkp_ref, wvt_ref, cos_ref,
                   sin_ref, naq_ref, nak_ref, navt_ref, gmu_ref, gmv_ref, gates_ref, qm_ref, km_ref, vmt_ref):
    x = x_ref[...]
    d = x.shape[-1]
    h = _rms(x, d) * gn_ref[...]
    h = h * (1.0 + sc_ref[...]) + sh_ref[...]
    hb = h.astype(BF16)

    def proj(c0, width):
        return _dot(hb, w_ref[:, c0:c0 + width])

    bd = bd_ref[...]
    naq_ref[...] = (_head64_norm(proj(C_NAQ, NA_WIDTH), bd) * qg_ref[...]).astype(BF16)
    nak_ref[...] = (_head64_norm(proj(C_NAK, NA_WIDTH), bd) * kg_ref[...]).astype(BF16)
    navt_ref[...] = _dot_t(wnavt_ref[...], hb).astype(BF16)

    gmu_ref[...] = _gelu(proj(C_GMU, GM_WIDTH)).astype(BF16)
    gv = _gelu(proj(C_GMV, GM_WIDTH))
    mu = jnp.mean(gv, axis=-1, keepdims=True)
    gc = gv - mu
    var = jnp.mean(gc * gc, axis=-1, keepdims=True)
    gmv_ref[...] = (gc * lax.rsqrt(var + NORM_EPS) * lng_ref[...] + lnb_ref[...]).astype(BF16)

    n_gate = gates_ref.shape[-1]
    gate_chunk = 512
    for j in range(n_gate // gate_chunk):
        cs = slice(j * gate_chunk, (j + 1) * gate_chunk)
        gates_ref[:, cs] = jax.nn.sigmoid(proj(C_GATES + j * gate_chunk, gate_chunk)).astype(BF16)

    ckv_n = (_rms(proj(C_CKV, MLA_KV_RANK), MLA_KV_RANK) * ckvg_ref[...]).astype(BF16)
    cq_n = (_rms(proj(C_CQ, MLA_Q_RANK), MLA_Q_RANK) * cqg_ref[...]).astype(BF16)
    kr = proj(C_KR, LANES)
    kr_hi = kr.astype(BF16)
    kr_lo = (kr - kr_hi.astype(F32)).astype(BF16)
    lane = lax.broadcasted_iota(jnp.int32, (1, LANES), 1)
    kr_hl = jnp.where(lane < MLA_ROPE, kr_hi, kr_lo)
    lhs_k = jnp.concatenate([ckv_n, kr_hl], axis=1)

    vmt_ref[...] = _dot_t(wvt_ref[...], ckv_n).astype(BF16)

    cosv = cos_ref[...]
    sinv = sin_ref[...]

    def heads(lhs, w_r, wp_r, g_r, gp_r, out_r):
        xa = _dot(lhs, w_r[...])
        xpa = _dot(lhs, wp_r[...])
        for hd in range(MLA_HEADS):
            sl = slice(hd * MLA_SLOT, (hd + 1) * MLA_SLOT)
            xh = xa[:, sl]
            r = lax.rsqrt(jnp.sum(xh * xh, axis=-1, keepdims=True) * (1.0 / MLA_QK) + NORM_EPS)
            out_r[:, sl] = (r * (xh * (g_r[:, sl] * cosv) + xpa[:, sl] * (gp_r[:, sl] * sinv))).astype(BF16)

    heads(cq_n, wq_ref, wqp_ref, gq_ref, gqp_ref, qm_ref)
    heads(lhs_k, wk_ref, wkp_ref, gk_ref, gkp_ref, km_ref)


def _in_projection(xs, mod_l, mod_row_fn, tm, tiles_per_seq, gn, lw, cos_t, sin_t):
    r, d = xs.shape
    n_tiles = r // tm
    row = lambda w: pl.BlockSpec((tm, w), lambda i: (i, 0))
    modspec = lambda chunk: pl.BlockSpec((None, None, 1, d), lambda i: (mod_row_fn(i), chunk, 0, 0))
    pos = pl.BlockSpec((tm, LANES), lambda i: (i % tiles_per_seq, 0))
    in_specs = [
        row(d), modspec(0), modspec(1), _resident((1, d)), _resident((d, C_TOTAL)), _resident((NA_WIDTH, d)),
        _resident((1, NA_WIDTH)), _resident((1, NA_WIDTH)), _resident((2 * LANES, 2 * LANES)),
        _resident((1, MLA_KV_RANK)), _resident((1, MLA_Q_RANK)), _resident((1, GM_WIDTH)), _resident((1, GM_WIDTH)),
        _resident((MLA_Q_RANK, MLA_HEADS * MLA_SLOT)), _resident((MLA_Q_RANK, MLA_HEADS * MLA_SLOT)),
        _resident((1, MLA_HEADS * MLA_SLOT)), _resident((1, MLA_HEADS * MLA_SLOT)),
        _resident((2 * LANES, MLA_HEADS * MLA_SLOT)), _resident((2 * LANES, MLA_HEADS * MLA_SLOT)),
        _resident((1, MLA_HEADS * MLA_SLOT)), _resident((1, MLA_HEADS * MLA_SLOT)),
        _resident((MLA_HEADS * MLA_V, MLA_KV_RANK)), pos, pos,
    ]
    outs = [("naq", NA_WIDTH, False), ("nak", NA_WIDTH, False), ("navt", NA_WIDTH, True), ("gmu", GM_WIDTH, False),
            ("gmv", GM_WIDTH, False), ("gates", N_BRANCHES * d, False), ("qm", MLA_HEADS * MLA_SLOT, False),
            ("km", MLA_HEADS * MLA_SLOT, False), ("vmt", MLA_HEADS * MLA_V, True)]
    col = lambda w: pl.BlockSpec((w, tm), lambda i: (0, i))
    return pl.pallas_call(
        _inproj_kernel,
        grid=(n_tiles,),
        in_specs=in_specs,
        out_specs=[col(w) if t else row(w) for _, w, t in outs],
        out_shape=[jax.ShapeDtypeStruct((w, r) if t else (r, w), BF16) for _, w, t in outs],
        compiler_params=_params(),
        name="in_projection",
    )(xs, mod_l, mod_l, gn, lw["w_in"], lw["w_navt"], lw["na_qg"], lw["na_kg"], lw["bd"], lw["ckv_g"], lw["cq_g"],
      lw["ln_g"],
      lw["ln_b"], lw["wq"], lw["wqp"], lw["gq"], lw["gqp"], lw["wk"], lw["wkp"], lw["gk"], lw["gkp"], lw["wvt"],
      cos_t, sin_t)


def _pair_scores(q_pair, k_blocks):
    lane = lax.broadcasted_iota(jnp.int32, (1, LANES), 1)
    zero = jnp.zeros_like(q_pair)
    q2 = jnp.concatenate([jnp.where(lane < NA_HEAD_DIM, q_pair, zero), jnp.where(lane >= NA_HEAD_DIM, q_pair, zero)],
                         axis=0)
    return [_dot_t(k_pair, q2) if bias_t is None else _dot_t(k_pair, q2) + bias_t for k_pair, bias_t in k_blocks]


def _pair_softmax_pv(sts, vt_blocks):
    nq = sts[0].shape[1] // 2
    m = None
    for st in sts:
        mb = jnp.max(st, axis=0, keepdims=True)
        m = mb if m is None else jnp.maximum(m, mb)
    acc = None
    for st, vt_pair in zip(sts, vt_blocks):
        pt = jnp.exp2(st - m).astype(BF16)
        ones = jnp.ones((BF16_SUBLANES, vt_pair.shape[1]), BF16)
        part = _dot(jnp.concatenate([vt_pair, ones], axis=0), pt)
        acc = part if acc is None else acc + part
    o0 = acc[:NA_HEAD_DIM, :nq] / acc[LANES:LANES + 1, :nq]
    o1 = acc[NA_HEAD_DIM:LANES, nq:] / acc[LANES:LANES + 1, nq:]
    return jnp.concatenate([o0, o1], axis=0).T


def _pairs_pipelined(n_pairs, scores_fn, finish_fn):
    all_scores = [scores_fn(p) for p in range(n_pairs)]
    for p in range(n_pairs):
        finish_fn(p, all_scores[p])


def _na_kernel(q_ref, k_ref, vt_ref, kc_ref, vct_ref, bias_ref, o_ref, *, grid_rows):
    rb = pl.program_id(1)
    band0 = jnp.clip(NA_QR * rb - NA_WIN_H // 2, 0, grid_rows - NA_BAND)
    k0 = pl.multiple_of(band0 * GRID_W, 2 * LANES)
    nk = NA_BAND * GRID_W
    cols = lambda hp: slice(hp * LANES, (hp + 1) * LANES)

    def scores(hp):
        cs = cols(hp)
        return _pair_scores(q_ref[:, cs], [(k_ref[pl.ds(k0, nk), cs], bias_ref[0, hp]), (kc_ref[:, cs], None)])

    def finish(hp, sts):
        cs = cols(hp)
        o_ref[:, cs] = _pair_softmax_pv(sts, [vt_ref[cs, pl.ds(k0, nk)], vct_ref[cs, :]]).astype(BF16)

    _pairs_pipelined(NA_HEADS // 2, scores, finish)


def _na_attention(q, k, vt, kc, vct, bias, layer, batch, seq, ctx_len):
    grid_rows = seq // GRID_W
    n_rb = grid_rows // NA_QR
    tq = NA_QR * GRID_W
    nk = NA_BAND * GRID_W

    def bias_cls(b, rb):
        return (layer, jnp.where(rb == 0, 0, jnp.where(rb == n_rb - 1, 2, 1)), 0, 0, 0)

    return pl.pallas_call(
        functools.partial(_na_kernel, grid_rows=grid_rows),
        grid=(batch, n_rb),
        in_specs=[
            pl.BlockSpec((tq, NA_WIDTH), lambda b, rb: (b * n_rb + rb, 0)),
            pl.BlockSpec((seq, NA_WIDTH), lambda b, rb: (b, 0)),
            pl.BlockSpec((NA_WIDTH, seq), lambda b, rb: (0, b)),
            pl.BlockSpec((ctx_len, NA_WIDTH), lambda b, rb: (b, 0)),
            pl.BlockSpec((NA_WIDTH, ctx_len), lambda b, rb: (0, b)),
            pl.BlockSpec((None, 1, NA_HEADS // 2, nk, 2 * tq), bias_cls),
        ],
        out_specs=pl.BlockSpec((tq, NA_WIDTH), lambda b, rb: (b * n_rb + rb, 0)),
        out_shape=jax.ShapeDtypeStruct((batch * seq, NA_WIDTH), BF16),
        compiler_params=_params(),
        name="na_attention",
    )(q, k, vt, kc, vct, bias)


NA_DR = 2 * NA_WIN_H - 1
NA_DC = 2 * NA_WIN_W - 1


def _na_window_geometry(grid_rows):
    n_rb = grid_rows // NA_QR
    kh, kw = NA_WIN_H, NA_WIN_W
    cq = np.arange(GRID_W)
    c0 = np.clip(cq - kw // 2, 0, GRID_W - kw)
    col_in = (cq[None, :] >= c0[:, None]) & (cq[None, :] < c0[:, None] + kw)
    dc = np.clip(cq[None, :] - cq[:, None], -(kw - 1), kw - 1) + kw - 1
    dc_t = np.tile(dc.T, (1, 2)).astype(np.int32)
    col_in_t = np.tile(col_in.T, (1, 2)).astype(np.int32)
    dr = np.zeros((3, NA_QR, NA_BAND), np.int32)
    ok = np.zeros((3, NA_QR, NA_BAND), bool)
    for c, rb in enumerate((0, 1, n_rb - 1)):
        band0 = int(np.clip(NA_QR * rb - kh // 2, 0, grid_rows - NA_BAND))
        for i in range(NA_QR):
            rq = NA_QR * rb + i
            r0 = int(np.clip(rq - kh // 2, 0, grid_rows - kh))
            for j in range(NA_BAND):
                kr = band0 + j
                ok[c, i, j] = r0 <= kr < r0 + kh
                dr[c, i, j] = int(np.clip(kr - rq + kh - 1, 0, NA_DR - 1))
    return dc_t, col_in_t, dr, ok


def _na_bias_kernel(rpb_ref, dc_ref, colin_ref, o_ref, tile_ref, *, dr, ok):
    layer = pl.program_id(0)
    dc = dc_ref[...]
    col_in = colin_ref[...] > 0
    n_tiles = NA_HEADS * NA_DR

    def build(t, carry):
        base = (layer * n_tiles + t) * NA_DC
        acc = jnp.zeros(dc.shape, F32)
        for d in range(NA_DC):
            acc = jnp.where(dc == d, rpb_ref[base + d], acc)
        tile_ref[t] = jnp.where(col_in, acc * LOG2E, NEG_INF)
        return carry

    lax.fori_loop(0, n_tiles, build, 0)

    lane = lax.broadcasted_iota(jnp.int32, (1, LANES), 1)
    neg = jnp.full((GRID_W, LANES), NEG_INF, F32)
    for c in range(3):
        for hp in range(NA_HEADS // 2):
            for j in range(NA_BAND):
                for g in range(2 * NA_QR // 2):
                    head = 2 * hp + g // (NA_QR // 2)
                    i0 = 2 * (g % (NA_QR // 2))
                    halves = [tile_ref[head * NA_DR + int(dr[c, i, j])] if ok[c, i, j] else neg for i in (i0, i0 + 1)]
                    o_ref[c, hp, j * GRID_W:(j + 1) * GRID_W, g * LANES:(g + 1) * LANES] = jnp.where(
                        lane < GRID_W, halves[0], halves[1])


def _na_bias_tables(na_rpb, grid_rows):
    depth = na_rpb.shape[0]
    dc_t, col_in_t, dr, ok = _na_window_geometry(grid_rows)
    nk, nq2 = NA_BAND * GRID_W, 2 * NA_QR * GRID_W
    return pl.pallas_call(
        functools.partial(_na_bias_kernel, dr=dr, ok=ok),
        grid=(depth,),
        in_specs=[
            pl.BlockSpec(memory_space=pltpu.SMEM),
            pl.BlockSpec((GRID_W, LANES), lambda l: (0, 0)),
            pl.BlockSpec((GRID_W, LANES), lambda l: (0, 0)),
        ],
        out_specs=pl.BlockSpec((None, 3, NA_HEADS // 2, nk, nq2), lambda l: (l, 0, 0, 0, 0)),
        out_shape=jax.ShapeDtypeStruct((depth, 3, NA_HEADS // 2, nk, nq2), F32),
        scratch_shapes=[pltpu.VMEM((NA_HEADS * NA_DR, GRID_W, LANES), F32)],
        compiler_params=_params(),
        name="na_bias_tables",
    )(na_rpb.reshape(-1), jnp.asarray(dc_t), jnp.asarray(col_in_t))


def _mla_kernel(q_ref, kc_ref, vct_ref, *rest, n_kb, tk):
    if n_kb:
        k_ref, vt_ref, o_ref = rest
    else:
        (o_ref,) = rest
    tq = q_ref.shape[0]
    row = lax.broadcasted_iota(jnp.int32, (2 * MLA_V, 1), 0)
    lo = row < MLA_V
    heads = (slice(0, MLA_SLOT), slice(MLA_SLOT, 2 * MLA_SLOT))
    q = tuple(q_ref[:, hs] for hs in heads)

    def scores(k_blk):
        return tuple(_dot_t(k_blk[:, heads[h]], q[h]) for h in range(2))

    def values(vt_blk):
        one = jnp.ones_like(vt_blk)
        return jnp.where(lo, vt_blk, one), jnp.where(lo, one, vt_blk)

    def update(st, m, acc, vt_aug):
        m_new = jnp.maximum(m, jnp.max(st, axis=0, keepdims=True))
        pt = jnp.exp2(st - m_new).astype(BF16)
        return m_new, jnp.exp2(m - m_new) * acc + _dot(vt_aug, pt)

    def process(st, state, vt_blk):
        vt_aug = values(vt_blk)
        out = ()
        for h in range(2):
            out += update(st[h], state[2 * h], state[2 * h + 1], vt_aug[h])
        return out

    m0 = jnp.full((1, tq), NEG_INF, F32)
    a0 = jnp.zeros((2 * MLA_V, tq), F32)
    state = (m0, a0, m0, a0)
    s_cur = scores(kc_ref[...])
    if n_kb:
        s_nxt = scores(k_ref[pl.ds(0, tk), :])
        state = process(s_cur, state, vct_ref[...])

        for j in range(n_kb - 1):
            s_cur, s_nxt = s_nxt, scores(k_ref[pl.ds((j + 1) * tk, tk), :])
            state = process(s_cur, state, vt_ref[:, pl.ds(j * tk, tk)])
        state = process(s_nxt, state, vt_ref[:, pl.ds((n_kb - 1) * tk, tk)])
    else:
        state = process(s_cur, state, vct_ref[...])
    o0 = state[1][:MLA_V] / state[1][MLA_V:]
    o1 = state[3][MLA_V:] / state[3][:MLA_V]
    o_ref[...] = jnp.concatenate([o0, o1], axis=0).T.astype(BF16)


def _mla_attention(q, kc, vct, k, vt, batch, q_len, ctx_len, kv_len, tq, tk):
    n_q = q_len // tq
    n_pairs = MLA_HEADS // 2
    n_kb = kv_len // tk if k is not None else 0
    in_specs = [
        pl.BlockSpec((tq, 2 * MLA_SLOT), lambda b, p, i: (b * n_q + i, p)),
        pl.BlockSpec((ctx_len, 2 * MLA_SLOT), lambda b, p, i: (b, p)),
        pl.BlockSpec((2 * MLA_V, ctx_len), lambda b, p, i: (p, b)),
    ]
    args = [q, kc, vct]
    if n_kb:
        in_specs += [
            pl.BlockSpec((kv_len, 2 * MLA_SLOT), lambda b, p, i: (b, p)),
            pl.BlockSpec((2 * MLA_V, kv_len), lambda b, p, i: (p, b)),
        ]
        args += [k, vt]
    return pl.pallas_call(
        functools.partial(_mla_kernel, n_kb=n_kb, tk=tk),
        grid=(batch, n_pairs, n_q),
        in_specs=in_specs,
        out_specs=pl.BlockSpec((tq, LANES), lambda b, p, i: (b * n_q + i, p)),
        out_shape=jax.ShapeDtypeStruct((batch * q_len, MLA_HEADS * MLA_V), BF16),
        compiler_params=_params(),
        name="mla_attention" if n_kb else "mla_ctx_attention",
    )(*args)


def _na_ctx_kernel(q_ref, k_ref, vt_ref, o_ref):
    cols = lambda hp: slice(hp * LANES, (hp + 1) * LANES)

    def scores(hp):
        return _pair_scores(q_ref[:, cols(hp)], [(k_ref[:, cols(hp)], None)])

    def finish(hp, sts):
        o_ref[:, cols(hp)] = _pair_softmax_pv(sts, [vt_ref[cols(hp), :]]).astype(BF16)

    _pairs_pipelined(NA_HEADS // 2, scores, finish)


def _na_ctx_attention(q, k, vt, batch, ctx_len):
    spec = pl.BlockSpec((ctx_len, NA_WIDTH), lambda b: (b, 0))
    return pl.pallas_call(
        _na_ctx_kernel,
        grid=(batch,),
        in_specs=[spec, spec, pl.BlockSpec((NA_WIDTH, ctx_len), lambda b: (0, b))],
        out_specs=spec,
        out_shape=jax.ShapeDtypeStruct((batch * ctx_len, NA_WIDTH), BF16),
        compiler_params=_params(),
        name="na_ctx_attention",
    )(q, k, vt)


FFN_HIDDEN_CHUNK = 512


def _merge_ffn_kernel(x_ref, yna_ref, ymla_ref, gmu_ref, gmv_ref, gates_ref, gt1_ref, sh2_ref, sc2_ref, gt2_ref,
                      gn2_ref, wna_ref, wgm_ref, wmla_ref, wout_ref, w1_ref, w2_ref, ws_ref, bsb_ref, o_ref,
                      ygm_ref):
    tm, d = x_ref.shape
    for c in range(tm // GM_CHUNK):
        rs = slice(c * GM_CHUNK, (c + 1) * GM_CHUNK)
        for g in range(GM_GROUPS):
            cs = slice(g * GM_GROUP_DIM, (g + 1) * GM_GROUP_DIM)
            mixed = _dot(ws_ref[g], gmv_ref[rs, cs]) + bsb_ref[:, cs]
            ygm_ref[rs, cs] = (gmu_ref[rs, cs].astype(F32) * mixed).astype(BF16)

    y = gates_ref[:, 0:d].astype(F32) * _dot(yna_ref[...], wna_ref[...])
    y = y + gates_ref[:, d:2 * d].astype(F32) * _dot(ygm_ref[...], wgm_ref[...])
    y = y + gates_ref[:, 2 * d:3 * d].astype(F32) * _dot(ymla_ref[...], wmla_ref[...])
    xn = x_ref[...] + gt1_ref[...] * _dot(y.astype(BF16), wout_ref[...])

    h2 = _rms(xn, d) * gn2_ref[...]
    h2 = (h2 * (1.0 + sc2_ref[...]) + sh2_ref[...]).astype(BF16)
    hidden = w1_ref.shape[1]
    acc = None
    for c in range(hidden // FFN_HIDDEN_CHUNK):
        cs = slice(c * FFN_HIDDEN_CHUNK, (c + 1) * FFN_HIDDEN_CHUNK)
        a = jnp.maximum(_dot(h2, w1_ref[:, cs]), 0.0)
        part = _dot((a * a).astype(BF16), w2_ref[cs, :])
        acc = part if acc is None else acc + part
    o_ref[...] = xn + gt2_ref[...] * acc


def _merge_ffn(xs, y_na, y_mla, gm_u, gm_v, gates, mod_l, mod_row_fn, tm, gn2, lw):
    r, d = xs.shape
    hidden = lw["w1"].shape[1]
    row = lambda w: pl.BlockSpec((tm, w), lambda i: (i, 0))
    modspec = lambda chunk: pl.BlockSpec((None, None, 1, d), lambda i: (mod_row_fn(i), chunk, 0, 0))
    in_specs = [
        row(d), row(NA_WIDTH), row(MLA_HEADS * MLA_V), row(GM_WIDTH), row(GM_WIDTH), row(N_BRANCHES * d),
        modspec(2), modspec(3), modspec(4), modspec(5), _resident((1, d)),
        _resident((NA_WIDTH, d)), _resident((GM_WIDTH, d)), _resident((MLA_HEADS * MLA_V, d)), _resident((d, d)),
        _resident((d, hidden)), _resident((hidden, d)), _resident((GM_GROUPS, GM_CHUNK, GM_CHUNK)),
        _resident((GM_CHUNK, GM_WIDTH)),
    ]
    return pl.pallas_call(
        _merge_ffn_kernel,
        grid=(r // tm,),
        in_specs=in_specs,
        out_specs=row(d),
        out_shape=jax.ShapeDtypeStruct((r, d), F32),
        scratch_shapes=[pltpu.VMEM((tm, GM_WIDTH), BF16)],
        compiler_params=_params(),
        name="merge_ffn",
    )(xs, y_na, y_mla, gm_u, gm_v, gates, mod_l, mod_l, mod_l, mod_l, gn2, lw["w_na_o"], lw["w_gm_o"],
      lw["w_mla_o"], lw["w_out"], lw["w1"], lw["w2"], lw["w_s"], lw["b_s"])


def _rope_partner():
    perm = np.arange(MLA_QK)
    for u in range(MLA_ROPE):
        seg, half, pair = u // (2 * ROPE_AXIS_PAIRS), (u // ROPE_AXIS_PAIRS) % 2, u % ROPE_AXIS_PAIRS
        perm[MLA_NOPE + u] = MLA_NOPE + seg * 2 * ROPE_AXIS_PAIRS + (1 - half) * ROPE_AXIS_PAIRS + pair
    tail = np.arange(MLA_QK) >= MLA_NOPE
    return perm, tail


def _pad_heads(t):
    pad = [(0, 0)] * (t.ndim - 1) + [(0, MLA_SLOT - MLA_QK)]
    t = jnp.pad(t, pad)
    return t.reshape(t.shape[:-2] + (MLA_HEADS * MLA_SLOT,))


def _rope_tables(seq):
    t = jnp.arange(seq)
    rows = (t // GRID_W).astype(F32)
    colsv = (t % GRID_W).astype(F32)
    freqs = ROPE_THETA ** (-jnp.arange(ROPE_AXIS_PAIRS, dtype=F32) / ROPE_AXIS_PAIRS)
    ang = jnp.stack([rows[:, None] * freqs, colsv[:, None] * freqs], axis=1)
    cos, sin = jnp.cos(ang), jnp.sin(ang)
    cos_t = jnp.stack([cos, cos], axis=2).reshape(seq, MLA_ROPE)
    sin_t = jnp.stack([-sin, sin], axis=2).reshape(seq, MLA_ROPE)
    ones = jnp.ones((seq, MLA_NOPE), F32)
    padc = jnp.ones((seq, MLA_SLOT - MLA_QK), F32)
    cos_full = jnp.concatenate([ones, cos_t, padc], axis=1)
    sin_full = jnp.concatenate([0 * ones, sin_t, 0 * padc], axis=1)
    return cos_full, sin_full


def _pack_layer(i, p, off):
    d = p["w_in"].shape[1]
    w = p["w_in"][i]
    kr = w[:, off["kr"]:off["kr"] + MLA_ROPE]
    w_in = jnp.concatenate([
        w[:, off["naq"]:off["naq"] + NA_WIDTH], w[:, off["nak"]:off["nak"] + NA_WIDTH],
        w[:, off["ckv"]:off["ckv"] + MLA_KV_RANK],
        kr, kr, jnp.zeros((d, LANES - 2 * MLA_ROPE), w.dtype),
        w[:, off["cq"]:off["cq"] + MLA_Q_RANK], w[:, off["gmu"]:off["gmu"] + GM_WIDTH],
        w[:, off["gmv"]:off["gmv"] + GM_WIDTH], w[:, off["gates"]:off["gates"] + N_BRANCHES * d],
    ], axis=1).astype(BF16)

    perm, tail = _rope_partner()
    q_scale = MLA_QK ** -0.5 * LOG2E
    w_uq = p["mla_w_uq"][i].reshape(MLA_Q_RANK, MLA_HEADS, MLA_QK)
    wq = _pad_heads(w_uq).astype(BF16)
    wqp = _pad_heads(w_uq[:, :, perm] * tail).astype(BF16)
    qg = p["mla_q_gain"][i]
    gq = jnp.tile(jnp.pad(qg, (0, MLA_SLOT - MLA_QK)), MLA_HEADS)[None] * q_scale
    gqp = jnp.tile(jnp.pad(qg[perm] * tail, (0, MLA_SLOT - MLA_QK)), MLA_HEADS)[None] * q_scale

    w_ukv = p["mla_w_ukv"][i].reshape(MLA_KV_RANK, MLA_HEADS, MLA_NOPE + MLA_V)
    k_nope = jnp.pad(w_ukv[:, :, :MLA_NOPE], ((0, 0), (0, 0), (0, MLA_SLOT - MLA_NOPE)))
    k_nope = k_nope.reshape(MLA_KV_RANK, MLA_HEADS * MLA_SLOT)
    sel = np.zeros((LANES, MLA_HEADS, MLA_SLOT), np.float32)
    selp = np.zeros((LANES, MLA_HEADS, MLA_SLOT), np.float32)
    for t in range(MLA_ROPE):
        for copy in range(2):
            sel[copy * MLA_ROPE + t, :, MLA_NOPE + t] = 1.0
            selp[copy * MLA_ROPE + t, :, perm[MLA_NOPE + t]] = 1.0
    wk = jnp.concatenate([k_nope, sel.reshape(LANES, -1)], axis=0).astype(BF16)
    wkp = jnp.concatenate([jnp.zeros_like(k_nope), selp.reshape(LANES, -1)], axis=0).astype(BF16)
    kg = p["mla_k_gain"][i]
    gk = jnp.tile(jnp.pad(kg, (0, MLA_SLOT - MLA_QK)), MLA_HEADS)[None]
    gkp = jnp.tile(jnp.pad(kg[perm] * tail, (0, MLA_SLOT - MLA_QK)), MLA_HEADS)[None]
    wvt = w_ukv[:, :, MLA_NOPE:].reshape(MLA_KV_RANK, MLA_HEADS * MLA_V).T.astype(BF16)

    blk = np.arange(2 * LANES) // NA_HEAD_DIM
    bd = jnp.asarray(blk[:, None] == blk[None, :], BF16)

    return {
        "w_in": w_in,
        "w_navt": w[:, off["nav"]:off["nav"] + NA_WIDTH].T.astype(BF16),
        "na_qg": jnp.tile(p["na_q_gain"][i], NA_HEADS)[None] * (NA_HEAD_DIM ** -0.5 * LOG2E),
        "na_kg": jnp.tile(p["na_k_gain"][i], NA_HEADS)[None],
        "bd": bd,
        "ckv_g": p["mla_ckv_gain"][i][None], "cq_g": p["mla_cq_gain"][i][None],
        "ln_g": p["gm_ln_g"][i][None], "ln_b": p["gm_ln_b"][i][None],
        "wq": wq, "wqp": wqp, "gq": gq, "gqp": gqp, "wk": wk, "wkp": wkp, "gk": gk, "gkp": gkp, "wvt": wvt,
        "w_na_o": p["na_w_o"][i].astype(BF16), "w_gm_o": p["gm_w_o"][i].astype(BF16),
        "w_mla_o": p["mla_w_o"][i].astype(BF16), "w_out": p["w_out"][i].astype(BF16),
        "w1": p["ffn_w1"][i].astype(BF16), "w2": p["ffn_w2"][i].astype(BF16),
        "w_s": p["gm_w_s"][i].astype(BF16),
        "b_s": jnp.repeat(p["gm_b_s"][i].T, GM_GROUP_DIM, axis=1),
    }


def _in_offsets(d):
    off = {"nak": 0}
    off["nav"] = off["nak"] + NA_WIDTH
    off["ckv"] = off["nav"] + NA_WIDTH
    off["kr"] = off["ckv"] + MLA_KV_RANK
    off["naq"] = off["kr"] + MLA_ROPE
    off["gmu"] = off["naq"] + NA_WIDTH
    off["gmv"] = off["gmu"] + GM_WIDTH
    off["cq"] = off["gmv"] + GM_WIDTH
    off["gates"] = off["cq"] + MLA_Q_RANK
    return off


def kernel(x, c, ctx, c_ctx, w_mod, b_mod, g_norm1, g_norm2, w_in, na_q_gain, na_k_gain, na_rpb, na_w_o, gm_ln_g,
           gm_ln_b, gm_w_s, gm_b_s, gm_w_o, mla_cq_gain, mla_ckv_gain, mla_w_uq, mla_w_ukv, mla_q_gain, mla_k_gain,
           mla_w_o, w_out, ffn_w1, ffn_w2):
    batch, seq, d = x.shape
    ctx_len = ctx.shape[1]
    depth = w_mod.shape[0]
    p = dict(w_in=w_in, na_q_gain=na_q_gain, na_k_gain=na_k_gain, na_w_o=na_w_o, gm_ln_g=gm_ln_g, gm_ln_b=gm_ln_b,
             gm_w_s=gm_w_s, gm_b_s=gm_b_s, gm_w_o=gm_w_o, mla_cq_gain=mla_cq_gain, mla_ckv_gain=mla_ckv_gain,
             mla_w_uq=mla_w_uq, mla_w_ukv=mla_w_ukv, mla_q_gain=mla_q_gain, mla_k_gain=mla_k_gain, mla_w_o=mla_w_o,
             w_out=w_out, ffn_w1=ffn_w1, ffn_w2=ffn_w2)
    off = _in_offsets(d)

    mod_rows = 16
    ctx_row = batch
    rows = jnp.concatenate([c, c_ctx[None], jnp.zeros((mod_rows - batch - 1, d), F32)], axis=0)
    mod = _modulation(rows, w_mod, b_mod).reshape(depth, mod_rows, 6, 1, d)

    cos_t, sin_t = _rope_tables(seq)
    cos_c = jnp.ones((ctx_len, LANES), F32)
    sin_c = jnp.zeros((ctx_len, LANES), F32)

    tm = min(512, seq)
    tmc = min(256, ctx_len)
    tiles_per_seq = seq // tm
    tiles_per_ctx = ctx_len // tmc
    lat_row = lambda i: i // tiles_per_seq
    ctx_row_fn = lambda i: ctx_row
    grid_rows = seq // GRID_W
    na_bias = _na_bias_tables(na_rpb, grid_rows)

    xs = x.reshape(batch * seq, d)
    cs = ctx.reshape(batch * ctx_len, d)
    for i in range(depth):
        last = i == depth - 1
        lw = _pack_layer(i, p, off)
        mod_l = mod[i]
        gn1 = g_norm1[i][None]
        gn2 = g_norm2[i][None]

        (naq, nak, nav, gmu, gmv, gates, qm, km, vm) = _in_projection(
            xs, mod_l, lat_row, tm, tiles_per_seq, gn1, lw, cos_t, sin_t)
        (naq_c, nak_c, nav_c, gmu_c, gmv_c, gates_c, qm_c, km_c, vm_c) = _in_projection(
            cs, mod_l, ctx_row_fn, tmc, tiles_per_ctx, gn1, lw, cos_c, sin_c)

        y_na = _na_attention(naq, nak, nav, nak_c, nav_c, na_bias, i, batch, seq, ctx_len)
        y_mla = _mla_attention(qm, km_c, vm_c, km, vm, batch, seq, ctx_len, seq, tq=min(1024, seq), tk=min(512, seq))
        xs_new = _merge_ffn(xs, y_na, y_mla, gmu, gmv, gates, mod_l, lat_row, tm, gn2, lw)

        if not last:
            y_na_c = _na_ctx_attention(naq_c, nak_c, nav_c, batch, ctx_len)
            y_mla_c = _mla_attention(qm_c, km_c, vm_c, None, None, batch, ctx_len, ctx_len, 0, tq=ctx_len, tk=0)
            cs = _merge_ffn(cs, y_na_c, y_mla_c, gmu_c, gmv_c, gates_c, mod_l, ctx_row_fn, tmc, gn2, lw)
        xs = xs_new
    return xs.reshape(batch, seq, d)
```

```python
import functools
import math

import numpy as np
import jax
import jax.numpy as jnp
from jax import lax
from jax.experimental import pallas as pl
from jax.experimental.pallas import tpu as pltpu

F32 = jnp.float32
BF16 = jnp.bfloat16

LANES = 128
BF16_SUBLANES = 16
VMEM_LIMIT_BYTES = 60 * 1024 * 1024

GRID_W = 64
N_BRANCHES = 3
NA_HEADS = 8
NA_HEAD_DIM = 64
NA_WIDTH = NA_HEADS * NA_HEAD_DIM
NA_WIN_H = 8
NA_WIN_W = 16
GM_GROUPS = 4
GM_CHUNK = 128
GM_GROUP_DIM = 128
GM_WIDTH = GM_GROUPS * GM_GROUP_DIM
MLA_HEADS = 8
MLA_Q_RANK = 256
MLA_KV_RANK = 128
MLA_NOPE = 64
MLA_ROPE = 32
MLA_V = 64
MLA_QK = MLA_NOPE + MLA_ROPE
ROPE_AXIS_PAIRS = MLA_ROPE // 4
ROPE_THETA = 10000.0
NORM_EPS = 1e-6
NEG_INF = -1e30
LOG2E = math.log2(math.e)

NA_QR = 4
NA_BAND = 12
MLA_SLOT = LANES


def _resident(shape):
    zeros = (0,) * len(shape)
    return pl.BlockSpec(shape, lambda *_: zeros, pipeline_mode=pl.Buffered(1))


def _params():
    return pltpu.CompilerParams(vmem_limit_bytes=VMEM_LIMIT_BYTES)


def _dot(a, b):
    return jnp.dot(a, b, preferred_element_type=F32)


def _dot_t(a, b):
    return lax.dot_general(a, b, (((1,), (1,)), ((), ())), preferred_element_type=F32)


def _mod_kernel(s_ref, w_ref, b_ref, o_ref):
    s = s_ref[...]
    s = s * jax.nn.sigmoid(s)
    o_ref[...] = _dot(s.astype(BF16), w_ref[...].astype(BF16)) + b_ref[...]


def _modulation(rows, w_mod, b_mod):
    depth, d, n = w_mod.shape
    r = rows.shape[0]
    tn = 1536
    return pl.pallas_call(
        _mod_kernel,
        grid=(depth, n // tn),
        in_specs=[
            pl.BlockSpec((r, d), lambda l, j: (0, 0)),
            pl.BlockSpec((None, d, tn), lambda l, j: (l, 0, j)),
            pl.BlockSpec((None, 1, tn), lambda l, j: (l, 0, j)),
        ],
        out_specs=pl.BlockSpec((None, r, tn), lambda l, j: (l, 0, j)),
        out_shape=jax.ShapeDtypeStruct((depth, r, n), F32),
        compiler_params=_params(),
        name="modulation",
    )(rows, w_mod, b_mod.reshape(depth, 1, n))


C_NAQ = 0
C_NAK = C_NAQ + NA_WIDTH
C_CKV = C_NAK + NA_WIDTH
C_KR = C_CKV + MLA_KV_RANK
C_CQ = C_KR + LANES
C_GMU = C_CQ + MLA_Q_RANK
C_GMV = C_GMU + GM_WIDTH
C_GATES = C_GMV + GM_WIDTH
C_TOTAL = C_GATES + N_BRANCHES * 1024


def _rms(t, width):
    return t * lax.rsqrt(jnp.sum(t * t, axis=-1, keepdims=True) * (1.0 / width) + NORM_EPS)


def _gelu(t):
    return 0.5 * t * (1.0 + lax.erf(t * math.sqrt(0.5)))


def _head64_norm(p, bd):
    p2 = (p * p).astype(BF16)
    half = 2 * LANES
    ss = jnp.concatenate([_dot(p2[:, :half], bd), _dot(p2[:, half:], bd)], axis=1)
    return p * lax.rsqrt(ss * (1.0 / NA_HEAD_DIM) + NORM_EPS)


def _inproj_kernel(x_ref, sh_ref, sc_ref, gn_ref, w_ref, wnavt_ref, qg_ref, kg_ref, bd_ref, ckvg_ref, cqg_ref, lng_ref,
                   lnb_ref, wq_ref, wqp_ref, gq_ref, gqp_ref, wk_ref, wkp_ref, gk_ref, gkp_ref, wvt_ref, cos_ref,
                   sin_ref, naq_ref, nak_ref, navt_ref, gmu_ref, gmv_ref, gates_ref, qm_ref, km_ref, vmt_ref):
    x = x_ref[...]
    d = x.shape[-1]
    h = _rms(x, d) * gn_ref[...]
    h = h * (1.0 + sc_ref[...]) + sh_ref[...]
    hb = h.astype(BF16)

    def proj(c0, width):
        return _dot(hb, w_ref[:, c0:c0 + width])

    bd = bd_ref[...]
    naq_ref[...] = (_head64_norm(proj(C_NAQ, NA_WIDTH), bd) * qg_ref[...]).astype(BF16)
    nak_ref[...] = (_head64_norm(proj(C_NAK, NA_WIDTH), bd) * kg_ref[...]).astype(BF16)
    navt_ref[...] = _dot_t(wnavt_ref[...], hb).astype(BF16)

    gmu_ref[...] = _gelu(proj(C_GMU, GM_WIDTH)).astype(BF16)
    gv = _gelu(proj(C_GMV, GM_WIDTH))
    mu = jnp.mean(gv, axis=-1, keepdims=True)
    gc = gv - mu
    var = jnp.mean(gc * gc, axis=-1, keepdims=True)
    gmv_ref[...] = (gc * lax.rsqrt(var + NORM_EPS) * lng_ref[...] + lnb_ref[...]).astype(BF16)

    n_gate = gates_ref.shape[-1]
    gate_chunk = 512
    for j in range(n_gate // gate_chunk):
        cs = slice(j * gate_chunk, (j + 1) * gate_chunk)
        gates_ref[:, cs] = jax.nn.sigmoid(proj(C_GATES + j * gate_chunk, gate_chunk)).astype(BF16)

    ckv_n = (_rms(proj(C_CKV, MLA_KV_RANK), MLA_KV_RANK) * ckvg_ref[...]).astype(BF16)
    cq_n = (_rms(proj(C_CQ, MLA_Q_RANK), MLA_Q_RANK) * cqg_ref[...]).astype(BF16)
    kr = proj(C_KR, LANES)
    kr_hi = kr.astype(BF16)
    kr_lo = (kr - kr_hi.astype(F32)).astype(BF16)
    lane = lax.broadcasted_iota(jnp.int32, (1, LANES), 1)
    kr_hl = jnp.where(lane < MLA_ROPE, kr_hi, kr_lo)
    lhs_k = jnp.concatenate([ckv_n, kr_hl], axis=1)

    vmt_ref[...] = _dot_t(wvt_ref[...], ckv_n).astype(BF16)

    cosv = cos_ref[...]
    sinv = sin_ref[...]

    def heads(lhs, w_r, wp_r, g_r, gp_r, out_r):
        xa = _dot(lhs, w_r[...])
        xpa = _dot(lhs, wp_r[...])
        for hd in range(MLA_HEADS):
            sl = slice(hd * MLA_SLOT, (hd + 1) * MLA_SLOT)
            xh = xa[:, sl]
            r = lax.rsqrt(jnp.sum(xh * xh, axis=-1, keepdims=True) * (1.0 / MLA_QK) + NORM_EPS)
            out_r[:, sl] = (r * (xh * (g_r[:, sl] * cosv) + xpa[:, sl] * (gp_r[:, sl] * sinv))).astype(BF16)

    heads(cq_n, wq_ref, wqp_ref, gq_ref, gqp_ref, qm_ref)
    heads(lhs_k, wk_ref, wkp_ref, gk_ref, gkp_ref, km_ref)


def _in_projection(xs, mod_l, mod_row_fn, tm, tiles_per_seq, gn, lw, cos_t, sin_t):
    r, d = xs.shape
    n_tiles = r // tm
    row = lambda w: pl.BlockSpec((tm, w), lambda i: (i, 0))
    modspec = lambda chunk: pl.BlockSpec((None, None, 1, d), lambda i: (mod_row_fn(i), chunk, 0, 0))
    pos = pl.BlockSpec((tm, LANES), lambda i: (i % tiles_per_seq, 0))
    in_specs = [
        row(d), modspec(0), modspec(1), _resident((1, d)), _resident((d, C_TOTAL)), _resident((NA_WIDTH, d)),
        _resident((1, NA_WIDTH)), _resident((1, NA_WIDTH)), _resident((2 * LANES, 2 * LANES)),
        _resident((1, MLA_KV_RANK)), _resident((1, MLA_Q_RANK)), _resident((1, GM_WIDTH)), _resident((1, GM_WIDTH)),
        _resident((MLA_Q_RANK, MLA_HEADS * MLA_SLOT)), _resident((MLA_Q_RANK, MLA_HEADS * MLA_SLOT)),
        _resident((1, MLA_HEADS * MLA_SLOT)), _resident((1, MLA_HEADS * MLA_SLOT)),
        _resident((2 * LANES, MLA_HEADS * MLA_SLOT)), _resident((2 * LANES, MLA_HEADS * MLA_SLOT)),
        _resident((1, MLA_HEADS * MLA_SLOT)), _resident((1, MLA_HEADS * MLA_SLOT)),
        _resident((MLA_HEADS * MLA_V, MLA_KV_RANK)), pos, pos,
    ]
    outs = [("naq", NA_WIDTH, False), ("nak", NA_WIDTH, False), ("navt", NA_WIDTH, True), ("gmu", GM_WIDTH, False),
            ("gmv", GM_WIDTH, False), ("gates", N_BRANCHES * d, False), ("qm", MLA_HEADS * MLA_SLOT, False),
            ("km", MLA_HEADS * MLA_SLOT, False), ("vmt", MLA_HEADS * MLA_V, True)]
    col = lambda w: pl.BlockSpec((w, tm), lambda i: (0, i))
    return pl.pallas_call(
        _inproj_kernel,
        grid=(n_tiles,),
        in_specs=in_specs,
        out_specs=[col(w) if t else row(w) for _, w, t in outs],
        out_shape=[jax.ShapeDtypeStruct((w, r) if t else (r, w), BF16) for _, w, t in outs],
        compiler_params=_params(),
        name="in_projection",
    )(xs, mod_l, mod_l, gn, lw["w_in"], lw["w_navt"], lw["na_qg"], lw["na_kg"], lw["bd"], lw["ckv_g"], lw["cq_g"],
      lw["ln_g"],
      lw["ln_b"], lw["wq"], lw["wqp"], lw["gq"], lw["gqp"], lw["wk"], lw["wkp"], lw["gk"], lw["gkp"], lw["wvt"],
      cos_t, sin_t)


def _pair_scores(q_pair, k_blocks):
    lane = lax.broadcasted_iota(jnp.int32, (1, LANES), 1)
    zero = jnp.zeros_like(q_pair)
    q2 = jnp.concatenate([jnp.where(lane < NA_HEAD_DIM, q_pair, zero), jnp.where(lane >= NA_HEAD_DIM, q_pair, zero)],
                         axis=0)
    return [_dot_t(k_pair, q2) if bias_t is None else _dot_t(k_pair, q2) + bias_t for k_pair, bias_t in k_blocks]


def _pair_softmax_pv(sts, vt_blocks):
    nq = sts[0].shape[1] // 2
    m = None
    for st in sts:
        mb = jnp.max(st, axis=0, keepdims=True)
        m = mb if m is None else jnp.maximum(m, mb)
    acc = None
    for st, vt_pair in zip(sts, vt_blocks):
        pt = jnp.exp2(st - m).astype(BF16)
        ones = jnp.ones((BF16_SUBLANES, vt_pair.shape[1]), BF16)
        part = _dot(jnp.concatenate([vt_pair, ones], axis=0), pt)
        acc = part if acc is None else acc + part
    o0 = acc[:NA_HEAD_DIM, :nq] / acc[LANES:LANES + 1, :nq]
    o1 = acc[NA_HEAD_DIM:LANES, nq:] / acc[LANES:LANES + 1, nq:]
    return jnp.concatenate([o0, o1], axis=0).T


def _pairs_pipelined(n_pairs, scores_fn, finish_fn):
    all_scores = [scores_fn(p) for p in range(n_pairs)]
    for p in range(n_pairs):
        finish_fn(p, all_scores[p])


def _na_kernel(q_ref, k_ref, vt_ref, kc_ref, vct_ref, bias_ref, o_ref, *, grid_rows):
    rb = pl.program_id(1)
    n_rb = grid_rows // NA_QR
    cls = jnp.where(rb == 0, 0, jnp.where(rb == n_rb - 1, 2, 1))
    band0 = jnp.clip(NA_QR * rb - NA_WIN_H // 2, 0, grid_rows - NA_BAND)
    k0 = pl.multiple_of(band0 * GRID_W, 2 * LANES)
    nk = NA_BAND * GRID_W
    cols = lambda hp: slice(hp * LANES, (hp + 1) * LANES)

    def scores(hp):
        cs = cols(hp)
        return _pair_scores(q_ref[:, cs], [(k_ref[pl.ds(k0, nk), cs], bias_ref[cls, hp]), (kc_ref[:, cs], None)])

    def finish(hp, sts):
        cs = cols(hp)
        o_ref[:, cs] = _pair_softmax_pv(sts, [vt_ref[cs, pl.ds(k0, nk)], vct_ref[cs, :]]).astype(BF16)

    _pairs_pipelined(NA_HEADS // 2, scores, finish)


def _na_attention(q, k, vt, kc, vct, bias, layer, batch, seq, ctx_len):
    grid_rows = seq // GRID_W
    n_rb = grid_rows // NA_QR
    tq = NA_QR * GRID_W
    nk = NA_BAND * GRID_W

    return pl.pallas_call(
        functools.partial(_na_kernel, grid_rows=grid_rows),
        grid=(batch, n_rb),
        in_specs=[
            pl.BlockSpec((tq, NA_WIDTH), lambda b, rb: (b * n_rb + rb, 0)),
            pl.BlockSpec((seq, NA_WIDTH), lambda b, rb: (b, 0)),
            pl.BlockSpec((NA_WIDTH, seq), lambda b, rb: (0, b)),
            pl.BlockSpec((ctx_len, NA_WIDTH), lambda b, rb: (b, 0)),
            pl.BlockSpec((NA_WIDTH, ctx_len), lambda b, rb: (0, b)),
            pl.BlockSpec((None, 3, NA_HEADS // 2, nk, 2 * tq), lambda b, rb: (layer, 0, 0, 0, 0),
                         pipeline_mode=pl.Buffered(1)),
        ],
        out_specs=pl.BlockSpec((tq, NA_WIDTH), lambda b, rb: (b * n_rb + rb, 0)),
        out_shape=jax.ShapeDtypeStruct((batch * seq, NA_WIDTH), BF16),
        compiler_params=_params(),
        name="na_attention",
    )(q, k, vt, kc, vct, bias)


NA_DR = 2 * NA_WIN_H - 1
NA_DC = 2 * NA_WIN_W - 1


def _na_window_geometry(grid_rows):
    n_rb = grid_rows // NA_QR
    kh, kw = NA_WIN_H, NA_WIN_W
    cq = np.arange(GRID_W)
    c0 = np.clip(cq - kw // 2, 0, GRID_W - kw)
    col_in = (cq[None, :] >= c0[:, None]) & (cq[None, :] < c0[:, None] + kw)
    dc = np.clip(cq[None, :] - cq[:, None], -(kw - 1), kw - 1) + kw - 1
    dc_t = np.tile(dc.T, (1, 2)).astype(np.int32)
    col_in_t = np.tile(col_in.T, (1, 2)).astype(np.int32)
    dr = np.zeros((3, NA_QR, NA_BAND), np.int32)
    ok = np.zeros((3, NA_QR, NA_BAND), bool)
    for c, rb in enumerate((0, 1, n_rb - 1)):
        band0 = int(np.clip(NA_QR * rb - kh // 2, 0, grid_rows - NA_BAND))
        for i in range(NA_QR):
            rq = NA_QR * rb + i
            r0 = int(np.clip(rq - kh // 2, 0, grid_rows - kh))
            for j in range(NA_BAND):
                kr = band0 + j
                ok[c, i, j] = r0 <= kr < r0 + kh
                dr[c, i, j] = int(np.clip(kr - rq + kh - 1, 0, NA_DR - 1))
    return dc_t, col_in_t, dr, ok


def _na_bias_kernel(rpb_ref, dc_ref, colin_ref, o_ref, tile_ref, *, dr, ok):
    layer = pl.program_id(0)
    dc = dc_ref[...]
    col_in = colin_ref[...] > 0
    n_tiles = NA_HEADS * NA_DR

    def build(t, carry):
        base = (layer * n_tiles + t) * NA_DC
        acc = jnp.zeros(dc.shape, F32)
        for d in range(NA_DC):
            acc = jnp.where(dc == d, rpb_ref[base + d], acc)
        tile_ref[t] = jnp.where(col_in, acc * LOG2E, NEG_INF)
        return carry

    lax.fori_loop(0, n_tiles, build, 0)

    lane = lax.broadcasted_iota(jnp.int32, (1, LANES), 1)
    neg = jnp.full((GRID_W, LANES), NEG_INF, F32)
    for c in range(3):
        for hp in range(NA_HEADS // 2):
            for j in range(NA_BAND):
                for g in range(2 * NA_QR // 2):
                    head = 2 * hp + g // (NA_QR // 2)
                    i0 = 2 * (g % (NA_QR // 2))
                    halves = [tile_ref[head * NA_DR + int(dr[c, i, j])] if ok[c, i, j] else neg for i in (i0, i0 + 1)]
                    o_ref[c, hp, j * GRID_W:(j + 1) * GRID_W, g * LANES:(g + 1) * LANES] = jnp.where(
                        lane < GRID_W, halves[0], halves[1])


def _na_bias_tables(na_rpb, grid_rows):
    depth = na_rpb.shape[0]
    dc_t, col_in_t, dr, ok = _na_window_geometry(grid_rows)
    nk, nq2 = NA_BAND * GRID_W, 2 * NA_QR * GRID_W
    return pl.pallas_call(
        functools.partial(_na_bias_kernel, dr=dr, ok=ok),
        grid=(depth,),
        in_specs=[
            pl.BlockSpec(memory_space=pltpu.SMEM),
            pl.BlockSpec((GRID_W, LANES), lambda l: (0, 0)),
            pl.BlockSpec((GRID_W, LANES), lambda l: (0, 0)),
        ],
        out_specs=pl.BlockSpec((None, 3, NA_HEADS // 2, nk, nq2), lambda l: (l, 0, 0, 0, 0)),
        out_shape=jax.ShapeDtypeStruct((depth, 3, NA_HEADS // 2, nk, nq2), F32),
        scratch_shapes=[pltpu.VMEM((NA_HEADS * NA_DR, GRID_W, LANES), F32)],
        compiler_params=_params(),
        name="na_bias_tables",
    )(na_rpb.reshape(-1), jnp.asarray(dc_t), jnp.asarray(col_in_t))


def _mla_kernel(q_ref, kc_ref, vct_ref, *rest, n_kb, tk):
    if n_kb:
        k_ref, vt_ref, o_ref = rest
    else:
        (o_ref,) = rest
    tq = q_ref.shape[0]
    heads = (slice(0, MLA_SLOT), slice(MLA_SLOT, 2 * MLA_SLOT))
    q = tuple(q_ref[:, hs] for hs in heads)

    def scores(k_blk):
        return tuple(_dot_t(k_blk[:, heads[h]], q[h]) for h in range(2))

    def values(vt_blk):
        ones = jnp.ones((BF16_SUBLANES, vt_blk.shape[1]), BF16)
        return jnp.concatenate([vt_blk[:MLA_V], ones], axis=0), jnp.concatenate([vt_blk[MLA_V:], ones], axis=0)

    def update(st, m, acc, vt_aug):
        m_new = jnp.maximum(m, jnp.max(st, axis=0, keepdims=True))
        pt = jnp.exp2(st - m_new).astype(BF16)
        return m_new, jnp.exp2(m - m_new) * acc + _dot(vt_aug, pt)

    def process(st, state, vt_blk):
        vt_aug = values(vt_blk)
        out = ()
        for h in range(2):
            out += update(st[h], state[2 * h], state[2 * h + 1], vt_aug[h])
        return out

    m0 = jnp.full((1, tq), NEG_INF, F32)
    a0 = jnp.zeros((MLA_V + BF16_SUBLANES, tq), F32)
    state = (m0, a0, m0, a0)
    s_cur = scores(kc_ref[...])
    if n_kb:
        s_nxt = scores(k_ref[pl.ds(0, tk), :])
        state = process(s_cur, state, vct_ref[...])

        for j in range(n_kb - 1):
            s_cur, s_nxt = s_nxt, scores(k_ref[pl.ds((j + 1) * tk, tk), :])
            state = process(s_cur, state, vt_ref[:, pl.ds(j * tk, tk)])
        state = process(s_nxt, state, vt_ref[:, pl.ds((n_kb - 1) * tk, tk)])
    else:
        state = process(s_cur, state, vct_ref[...])
    o0 = state[1][:MLA_V] / state[1][MLA_V:MLA_V + 1]
    o1 = state[3][:MLA_V] / state[3][MLA_V:MLA_V + 1]
    o_ref[...] = jnp.concatenate([o0, o1], axis=0).T.astype(BF16)


def _mla_attention(q, kc, vct, k, vt, batch, q_len, ctx_len, kv_len, tq, tk):
    n_q = q_len // tq
    n_pairs = MLA_HEADS // 2
    n_kb = kv_len // tk if k is not None else 0
    in_specs = [
        pl.BlockSpec((tq, 2 * MLA_SLOT), lambda b, p, i: (b * n_q + i, p)),
        pl.BlockSpec((ctx_len, 2 * MLA_SLOT), lambda b, p, i: (b, p)),
        pl.BlockSpec((2 * MLA_V, ctx_len), lambda b, p, i: (p, b)),
    ]
    args = [q, kc, vct]
    if n_kb:
        in_specs += [
            pl.BlockSpec((kv_len, 2 * MLA_SLOT), lambda b, p, i: (b, p)),
            pl.BlockSpec((2 * MLA_V, kv_len), lambda b, p, i: (p, b)),
        ]
        args += [k, vt]
    return pl.pallas_call(
        functools.partial(_mla_kernel, n_kb=n_kb, tk=tk),
        grid=(batch, n_pairs, n_q),
        in_specs=in_specs,
        out_specs=pl.BlockSpec((tq, LANES), lambda b, p, i: (b * n_q + i, p)),
        out_shape=jax.ShapeDtypeStruct((batch * q_len, MLA_HEADS * MLA_V), BF16),
        compiler_params=_params(),
        name="mla_attention" if n_kb else "mla_ctx_attention",
    )(*args)


def _na_ctx_kernel(q_ref, k_ref, vt_ref, o_ref):
    cols = lambda hp: slice(hp * LANES, (hp + 1) * LANES)

    def scores(hp):
        return _pair_scores(q_ref[:, cols(hp)], [(k_ref[:, cols(hp)], None)])

    def finish(hp, sts):
        o_ref[:, cols(hp)] = _pair_softmax_pv(sts, [vt_ref[cols(hp), :]]).astype(BF16)

    _pairs_pipelined(NA_HEADS // 2, scores, finish)


def _na_ctx_attention(q, k, vt, batch, ctx_len):
    spec = pl.BlockSpec((ctx_len, NA_WIDTH), lambda b: (b, 0))
    return pl.pallas_call(
        _na_ctx_kernel,
        grid=(batch,),
        in_specs=[spec, spec, pl.BlockSpec((NA_WIDTH, ctx_len), lambda b: (0, b))],
        out_specs=spec,
        out_shape=jax.ShapeDtypeStruct((batch * ctx_len, NA_WIDTH), BF16),
        compiler_params=_params(),
        name="na_ctx_attention",
    )(q, k, vt)


FFN_HIDDEN_CHUNK = 512


def _merge_ffn_kernel(x_ref, yna_ref, ymla_ref, gmu_ref, gmv_ref, gates_ref, gt1_ref, sh2_ref, sc2_ref, gt2_ref,
                      gn2_ref, wna_ref, wgm_ref, wmla_ref, wout_ref, w1_ref, w2_ref, ws_ref, bsb_ref, o_ref,
                      ygm_ref):
    tm, d = x_ref.shape
    for c in range(tm // GM_CHUNK):
        rs = slice(c * GM_CHUNK, (c + 1) * GM_CHUNK)
        for g in range(GM_GROUPS):
            cs = slice(g * GM_GROUP_DIM, (g + 1) * GM_GROUP_DIM)
            mixed = _dot(ws_ref[g], gmv_ref[rs, cs]) + bsb_ref[:, cs]
            ygm_ref[rs, cs] = (gmu_ref[rs, cs].astype(F32) * mixed).astype(BF16)

    y = gates_ref[:, 0:d].astype(F32) * _dot(yna_ref[...], wna_ref[...])
    y = y + gates_ref[:, d:2 * d].astype(F32) * _dot(ygm_ref[...], wgm_ref[...])
    y = y + gates_ref[:, 2 * d:3 * d].astype(F32) * _dot(ymla_ref[...], wmla_ref[...])
    xn = x_ref[...] + gt1_ref[...] * _dot(y.astype(BF16), wout_ref[...])

    h2 = _rms(xn, d) * gn2_ref[...]
    h2 = (h2 * (1.0 + sc2_ref[...]) + sh2_ref[...]).astype(BF16)
    hidden = w1_ref.shape[1]
    acc = None
    for c in range(hidden // FFN_HIDDEN_CHUNK):
        cs = slice(c * FFN_HIDDEN_CHUNK, (c + 1) * FFN_HIDDEN_CHUNK)
        a = jnp.maximum(_dot(h2, w1_ref[:, cs]), 0.0)
        part = _dot((a * a).astype(BF16), w2_ref[cs, :])
        acc = part if acc is None else acc + part
    o_ref[...] = xn + gt2_ref[...] * acc


def _merge_ffn(xs, y_na, y_mla, gm_u, gm_v, gates, mod_l, mod_row_fn, tm, gn2, lw):
    r, d = xs.shape
    hidden = lw["w1"].shape[1]
    row = lambda w: pl.BlockSpec((tm, w), lambda i: (i, 0))
    modspec = lambda chunk: pl.BlockSpec((None, None, 1, d), lambda i: (mod_row_fn(i), chunk, 0, 0))
    in_specs = [
        row(d), row(NA_WIDTH), row(MLA_HEADS * MLA_V), row(GM_WIDTH), row(GM_WIDTH), row(N_BRANCHES * d),
        modspec(2), modspec(3), modspec(4), modspec(5), _resident((1, d)),
        _resident((NA_WIDTH, d)), _resident((GM_WIDTH, d)), _resident((MLA_HEADS * MLA_V, d)), _resident((d, d)),
        _resident((d, hidden)), _resident((hidden, d)), _resident((GM_GROUPS, GM_CHUNK, GM_CHUNK)),
        _resident((GM_CHUNK, GM_WIDTH)),
    ]
    return pl.pallas_call(
        _merge_ffn_kernel,
        grid=(r // tm,),
        in_specs=in_specs,
        out_specs=row(d),
        out_shape=jax.ShapeDtypeStruct((r, d), F32),
        scratch_shapes=[pltpu.VMEM((tm, GM_WIDTH), BF16)],
        compiler_params=_params(),
        name="merge_ffn",
    )(xs, y_na, y_mla, gm_u, gm_v, gates, mod_l, mod_l, mod_l, mod_l, gn2, lw["w_na_o"], lw["w_gm_o"],
      lw["w_mla_o"], lw["w_out"], lw["w1"], lw["w2"], lw["w_s"], lw["b_s"])


def _rope_partner():
    perm = np.arange(MLA_QK)
    for u in range(MLA_ROPE):
        seg, half, pair = u // (2 * ROPE_AXIS_PAIRS), (u // ROPE_AXIS_PAIRS) % 2, u % ROPE_AXIS_PAIRS
        perm[MLA_NOPE + u] = MLA_NOPE + seg * 2 * ROPE_AXIS_PAIRS + (1 - half) * ROPE_AXIS_PAIRS + pair
    tail = np.arange(MLA_QK) >= MLA_NOPE
    return perm, tail


def _pad_heads(t):
    pad = [(0, 0)] * (t.ndim - 1) + [(0, MLA_SLOT - MLA_QK)]
    t = jnp.pad(t, pad)
    return t.reshape(t.shape[:-2] + (MLA_HEADS * MLA_SLOT,))


def _rope_tables(seq):
    t = jnp.arange(seq)
    rows = (t // GRID_W).astype(F32)
    colsv = (t % GRID_W).astype(F32)
    freqs = ROPE_THETA ** (-jnp.arange(ROPE_AXIS_PAIRS, dtype=F32) / ROPE_AXIS_PAIRS)
    ang = jnp.stack([rows[:, None] * freqs, colsv[:, None] * freqs], axis=1)
    cos, sin = jnp.cos(ang), jnp.sin(ang)
    cos_t = jnp.stack([cos, cos], axis=2).reshape(seq, MLA_ROPE)
    sin_t = jnp.stack([-sin, sin], axis=2).reshape(seq, MLA_ROPE)
    ones = jnp.ones((seq, MLA_NOPE), F32)
    padc = jnp.ones((seq, MLA_SLOT - MLA_QK), F32)
    cos_full = jnp.concatenate([ones, cos_t, padc], axis=1)
    sin_full = jnp.concatenate([0 * ones, sin_t, 0 * padc], axis=1)
    return cos_full, sin_full


def _pack_layer(i, p, off):
    d = p["w_in"].shape[1]
    w = p["w_in"][i]
    kr = w[:, off["kr"]:off["kr"] + MLA_ROPE]
    w_in = jnp.concatenate([
        w[:, off["naq"]:off["naq"] + NA_WIDTH], w[:, off["nak"]:off["nak"] + NA_WIDTH],
        w[:, off["ckv"]:off["ckv"] + MLA_KV_RANK],
        kr, kr, jnp.zeros((d, LANES - 2 * MLA_ROPE), w.dtype),
        w[:, off["cq"]:off["cq"] + MLA_Q_RANK], w[:, off["gmu"]:off["gmu"] + GM_WIDTH],
        w[:, off["gmv"]:off["gmv"] + GM_WIDTH], w[:, off["gates"]:off["gates"] + N_BRANCHES * d],
    ], axis=1).astype(BF16)

    perm, tail = _rope_partner()
    q_scale = MLA_QK ** -0.5 * LOG2E
    w_uq = p["mla_w_uq"][i].reshape(MLA_Q_RANK, MLA_HEADS, MLA_QK)
    wq = _pad_heads(w_uq).astype(BF16)
    wqp = _pad_heads(w_uq[:, :, perm] * tail).astype(BF16)
    qg = p["mla_q_gain"][i]
    gq = jnp.tile(jnp.pad(qg, (0, MLA_SLOT - MLA_QK)), MLA_HEADS)[None] * q_scale
    gqp = jnp.tile(jnp.pad(qg[perm] * tail, (0, MLA_SLOT - MLA_QK)), MLA_HEADS)[None] * q_scale

    w_ukv = p["mla_w_ukv"][i].reshape(MLA_KV_RANK, MLA_HEADS, MLA_NOPE + MLA_V)
    k_nope = jnp.pad(w_ukv[:, :, :MLA_NOPE], ((0, 0), (0, 0), (0, MLA_SLOT - MLA_NOPE)))
    k_nope = k_nope.reshape(MLA_KV_RANK, MLA_HEADS * MLA_SLOT)
    sel = np.zeros((LANES, MLA_HEADS, MLA_SLOT), np.float32)
    selp = np.zeros((LANES, MLA_HEADS, MLA_SLOT), np.float32)
    for t in range(MLA_ROPE):
        for copy in range(2):
            sel[copy * MLA_ROPE + t, :, MLA_NOPE + t] = 1.0
            selp[copy * MLA_ROPE + t, :, perm[MLA_NOPE + t]] = 1.0
    wk = jnp.concatenate([k_nope, sel.reshape(LANES, -1)], axis=0).astype(BF16)
    wkp = jnp.concatenate([jnp.zeros_like(k_nope), selp.reshape(LANES, -1)], axis=0).astype(BF16)
    kg = p["mla_k_gain"][i]
    gk = jnp.tile(jnp.pad(kg, (0, MLA_SLOT - MLA_QK)), MLA_HEADS)[None]
    gkp = jnp.tile(jnp.pad(kg[perm] * tail, (0, MLA_SLOT - MLA_QK)), MLA_HEADS)[None]
    wvt = w_ukv[:, :, MLA_NOPE:].reshape(MLA_KV_RANK, MLA_HEADS * MLA_V).T.astype(BF16)

    blk = np.arange(2 * LANES) // NA_HEAD_DIM
    bd = jnp.asarray(blk[:, None] == blk[None, :], BF16)

    return {
        "w_in": w_in,
        "w_navt": w[:, off["nav"]:off["nav"] + NA_WIDTH].T.astype(BF16),
        "na_qg": jnp.tile(p["na_q_gain"][i], NA_HEADS)[None] * (NA_HEAD_DIM ** -0.5 * LOG2E),
        "na_kg": jnp.tile(p["na_k_gain"][i], NA_HEADS)[None],
        "bd": bd,
        "ckv_g": p["mla_ckv_gain"][i][None], "cq_g": p["mla_cq_gain"][i][None],
        "ln_g": p["gm_ln_g"][i][None], "ln_b": p["gm_ln_b"][i][None],
        "wq": wq, "wqp": wqp, "gq": gq, "gqp": gqp, "wk": wk, "wkp": wkp, "gk": gk, "gkp": gkp, "wvt": wvt,
        "w_na_o": p["na_w_o"][i].astype(BF16), "w_gm_o": p["gm_w_o"][i].astype(BF16),
        "w_mla_o": p["mla_w_o"][i].astype(BF16), "w_out": p["w_out"][i].astype(BF16),
        "w1": p["ffn_w1"][i].astype(BF16), "w2": p["ffn_w2"][i].astype(BF16),
        "w_s": p["gm_w_s"][i].astype(BF16),
        "b_s": jnp.repeat(p["gm_b_s"][i].T, GM_GROUP_DIM, axis=1),
    }


def _in_offsets(d):
    off = {"nak": 0}
    off["nav"] = off["nak"] + NA_WIDTH
    off["ckv"] = off["nav"] + NA_WIDTH
    off["kr"] = off["ckv"] + MLA_KV_RANK
    off["naq"] = off["kr"] + MLA_ROPE
    off["gmu"] = off["naq"] + NA_WIDTH
    off["gmv"] = off["gmu"] + GM_WIDTH
    off["cq"] = off["gmv"] + GM_WIDTH
    off["gates"] = off["cq"] + MLA_Q_RANK
    return off


def kernel(x, c, ctx, c_ctx, w_mod, b_mod, g_norm1, g_norm2, w_in, na_q_gain, na_k_gain, na_rpb, na_w_o, gm_ln_g,
           gm_ln_b, gm_w_s, gm_b_s, gm_w_o, mla_cq_gain, mla_ckv_gain, mla_w_uq, mla_w_ukv, mla_q_gain, mla_k_gain,
           mla_w_o, w_out, ffn_w1, ffn_w2):
    batch, seq, d = x.shape
    ctx_len = ctx.shape[1]
    depth = w_mod.shape[0]
    p = dict(w_in=w_in.astype(BF16), na_q_gain=na_q_gain, na_k_gain=na_k_gain, na_w_o=na_w_o, gm_ln_g=gm_ln_g, gm_ln_b=gm_ln_b,
             gm_w_s=gm_w_s, gm_b_s=gm_b_s, gm_w_o=gm_w_o, mla_cq_gain=mla_cq_gain, mla_ckv_gain=mla_ckv_gain,
             mla_w_uq=mla_w_uq, mla_w_ukv=mla_w_ukv, mla_q_gain=mla_q_gain, mla_k_gain=mla_k_gain, mla_w_o=mla_w_o,
             w_out=w_out, ffn_w1=ffn_w1, ffn_w2=ffn_w2)
    off = _in_offsets(d)

    mod_rows = 16
    ctx_row = batch
    rows = jnp.concatenate([c, c_ctx[None], jnp.zeros((mod_rows - batch - 1, d), F32)], axis=0)
    mod = _modulation(rows, w_mod, b_mod).reshape(depth, mod_rows, 6, 1, d)

    cos_t, sin_t = _rope_tables(seq)
    cos_c = jnp.ones((ctx_len, LANES), F32)
    sin_c = jnp.zeros((ctx_len, LANES), F32)

    tm = min(512, seq)
    tmc = min(256, ctx_len)
    tiles_per_seq = seq // tm
    tiles_per_ctx = ctx_len // tmc
    lat_row = lambda i: i // tiles_per_seq
    ctx_row_fn = lambda i: ctx_row
    grid_rows = seq // GRID_W
    na_bias = _na_bias_tables(na_rpb, grid_rows)

    xs = x.reshape(batch * seq, d)
    cs = ctx.reshape(batch * ctx_len, d)
    for i in range(depth):
        last = i == depth - 1
        lw = _pack_layer(i, p, off)
        mod_l = mod[i]
        gn1 = g_norm1[i][None]
        gn2 = g_norm2[i][None]

        (naq, nak, nav, gmu, gmv, gates, qm, km, vm) = _in_projection(
            xs, mod_l, lat_row, tm, tiles_per_seq, gn1, lw, cos_t, sin_t)
        (naq_c, nak_c, nav_c, gmu_c, gmv_c, gates_c, qm_c, km_c, vm_c) = _in_projection(
            cs, mod_l, ctx_row_fn, tmc, tiles_per_ctx, gn1, lw, cos_c, sin_c)

        y_na = _na_attention(naq, nak, nav, nak_c, nav_c, na_bias, i, batch, seq, ctx_len)
        y_mla = _mla_attention(qm, km_c, vm_c, km, vm, batch, seq, ctx_len, seq, tq=min(1024, seq), tk=min(512, seq))
        xs_new = _merge_ffn(xs, y_na, y_mla, gmu, gmv, gates, mod_l, lat_row, tm, gn2, lw)

        if not last:
            y_na_c = _na_ctx_attention(naq_c, nak_c, nav_c, batch, ctx_len)
            y_mla_c = _mla_attention(qm_c, km_c, vm_c, None, None, batch, ctx_len, ctx_len, 0, tq=ctx_len, tk=0)
            cs = _merge_ffn(cs, y_na_c, y_mla_c, gmu_c, gmv_c, gates_c, mod_l, ctx_row_fn, tmc, gn2, lw)
        xs = xs_new
    return xs.reshape(batch, seq, d)
```

```python
import functools
import math

import numpy as np
import jax
import jax.numpy as jnp
from jax import lax
from jax.experimental import pallas as pl
from jax.experimental.pallas import tpu as pltpu

F32 = jnp.float32
BF16 = jnp.bfloat16

LANES = 128
BF16_SUBLANES = 16
VMEM_LIMIT_BYTES = 60 * 1024 * 1024

GRID_W = 64
N_BRANCHES = 3
NA_HEADS = 8
NA_HEAD_DIM = 64
NA_WIDTH = NA_HEADS * NA_HEAD_DIM
NA_WIN_H = 8
NA_WIN_W = 16
GM_GROUPS = 4
GM_CHUNK = 128
GM_GROUP_DIM = 128
GM_WIDTH = GM_GROUPS * GM_GROUP_DIM
MLA_HEADS = 8
MLA_Q_RANK = 256
MLA_KV_RANK = 128
MLA_NOPE = 64
MLA_ROPE = 32
MLA_V = 64
MLA_QK = MLA_NOPE + MLA_ROPE
ROPE_AXIS_PAIRS = MLA_ROPE // 4
ROPE_THETA = 10000.0
NORM_EPS = 1e-6
NEG_INF = -1e30
LOG2E = math.log2(math.e)

NA_QR = 4
NA_BAND = 12
MLA_SLOT = LANES
ROW_TILE = 512
CTX_ROW_TILE = 256
MLA_TQ = 1024
MLA_TK = 512


def _resident(shape):
    zeros = (0,) * len(shape)
    return pl.BlockSpec(shape, lambda *_: zeros, pipeline_mode=pl.Buffered(1))


def _params():
    return pltpu.CompilerParams(vmem_limit_bytes=VMEM_LIMIT_BYTES)


def _dot(a, b):
    return jnp.dot(a, b, preferred_element_type=F32)


def _dot_t(a, b):
    return lax.dot_general(a, b, (((1,), (1,)), ((), ())), preferred_element_type=F32)


def _mod_kernel(s_ref, w_ref, b_ref, o_ref):
    s = s_ref[...]
    s = s * jax.nn.sigmoid(s)
    o_ref[...] = _dot(s.astype(BF16), w_ref[...].astype(BF16)) + b_ref[...]


def _modulation(rows, w_mod, b_mod):
    depth, d, n = w_mod.shape
    r = rows.shape[0]
    tn = 1536
    return pl.pallas_call(
        _mod_kernel,
        grid=(depth, n // tn),
        in_specs=[
            pl.BlockSpec((r, d), lambda l, j: (0, 0)),
            pl.BlockSpec((None, d, tn), lambda l, j: (l, 0, j)),
            pl.BlockSpec((None, 1, tn), lambda l, j: (l, 0, j)),
        ],
        out_specs=pl.BlockSpec((None, r, tn), lambda l, j: (l, 0, j)),
        out_shape=jax.ShapeDtypeStruct((depth, r, n), F32),
        compiler_params=_params(),
        name="modulation",
    )(rows, w_mod, b_mod.reshape(depth, 1, n))


C_NAQ = 0
C_NAK = C_NAQ + NA_WIDTH
C_CKV = C_NAK + NA_WIDTH
C_KR = C_CKV + MLA_KV_RANK
C_CQ = C_KR + LANES
C_GMU = C_CQ + MLA_Q_RANK
C_GMV = C_GMU + GM_WIDTH
C_GATES = C_GMV + GM_WIDTH
C_TOTAL = C_GATES + N_BRANCHES * 1024


def _rms(t, width):
    return t * lax.rsqrt(jnp.sum(t * t, axis=-1, keepdims=True) * (1.0 / width) + NORM_EPS)


def _gelu(t):
    return 0.5 * t * (1.0 + lax.erf(t * math.sqrt(0.5)))


def _sigmoid(t):
    return 0.5 * jnp.tanh(0.5 * t) + 0.5


def _head64_norm(p, bd):
    p2 = (p * p).astype(BF16)
    half = 2 * LANES
    ss = jnp.concatenate([_dot(p2[:, :half], bd), _dot(p2[:, half:], bd)], axis=1)
    return p * lax.rsqrt(ss * (1.0 / NA_HEAD_DIM) + NORM_EPS)


def _inproj_kernel(x_ref, sh_ref, sc_ref, gn_ref, w_ref, wnavt_ref, qg_ref, kg_ref, bd_ref, ckvg_ref, cqg_ref, lng_ref,
                   lnb_ref, wq_ref, wqp_ref, gq_ref, gqp_ref, wk_ref, wkp_ref, gk_ref, gkp_ref, wvt_ref, cos_ref,
                   sin_ref, naq_ref, nak_ref, navt_ref, gmu_ref, gmv_ref, gates_ref, qm_ref, km_ref, vmt_ref):
    x = x_ref[...]
    d = x.shape[-1]
    h = _rms(x, d) * gn_ref[...]
    h = h * (1.0 + sc_ref[...]) + sh_ref[...]
    hb = h.astype(BF16)

    def proj(c0, width):
        return _dot(hb, w_ref[:, c0:c0 + width])

    p_ckv = proj(C_CKV, MLA_KV_RANK)
    p_cq = proj(C_CQ, MLA_Q_RANK)
    kr = proj(C_KR, LANES)
    p_naq = proj(C_NAQ, NA_WIDTH)
    p_nak = proj(C_NAK, NA_WIDTH)
    navt_ref[...] = _dot_t(wnavt_ref[...], hb).astype(BF16)

    ckv_n = (_rms(p_ckv, MLA_KV_RANK) * ckvg_ref[...]).astype(BF16)
    cq_n = (_rms(p_cq, MLA_Q_RANK) * cqg_ref[...]).astype(BF16)
    kr_hi = kr.astype(BF16)
    kr_lo = (kr - kr_hi.astype(F32)).astype(BF16)
    lane = lax.broadcasted_iota(jnp.int32, (1, LANES), 1)
    kr_hl = jnp.where(lane < MLA_ROPE, kr_hi, kr_lo)
    lhs_k = jnp.concatenate([ckv_n, kr_hl], axis=1)

    p_gmu = proj(C_GMU, GM_WIDTH)
    p_gmv = proj(C_GMV, GM_WIDTH)

    bd = bd_ref[...]
    naq_ref[...] = (_head64_norm(p_naq, bd) * qg_ref[...]).astype(BF16)
    nak_ref[...] = (_head64_norm(p_nak, bd) * kg_ref[...]).astype(BF16)

    vmt_ref[...] = _dot_t(wvt_ref[...], ckv_n).astype(BF16)
    xq, xqp = _dot(cq_n, wq_ref[...]), _dot(cq_n, wqp_ref[...])
    xk, xkp = _dot(lhs_k, wk_ref[...]), _dot(lhs_k, wkp_ref[...])

    gmu_ref[...] = _gelu(p_gmu).astype(BF16)
    gv = _gelu(p_gmv)
    mu = jnp.mean(gv, axis=-1, keepdims=True)
    gc = gv - mu
    var = jnp.mean(gc * gc, axis=-1, keepdims=True)
    gmv_ref[...] = (gc * lax.rsqrt(var + NORM_EPS) * lng_ref[...] + lnb_ref[...]).astype(BF16)

    n_gate = gates_ref.shape[-1]
    gate_chunk = 512
    p_gate = proj(C_GATES, gate_chunk)

    cosv = cos_ref[...]
    sinv = sin_ref[...]

    def heads(xa, xpa, g_r, gp_r, out_r):
        for hd in range(MLA_HEADS):
            sl = slice(hd * MLA_SLOT, (hd + 1) * MLA_SLOT)
            xh = xa[:, sl]
            r = lax.rsqrt(jnp.sum(xh * xh, axis=-1, keepdims=True) * (1.0 / MLA_QK) + NORM_EPS)
            out_r[:, sl] = (r * (xh * (g_r[:, sl] * cosv) + xpa[:, sl] * (gp_r[:, sl] * sinv))).astype(BF16)

    heads(xq, xqp, gq_ref, gqp_ref, qm_ref)
    heads(xk, xkp, gk_ref, gkp_ref, km_ref)

    for j in range(n_gate // gate_chunk):
        nxt = proj(C_GATES + (j + 1) * gate_chunk, gate_chunk) if (j + 1) * gate_chunk < n_gate else None
        gates_ref[:, j * gate_chunk:(j + 1) * gate_chunk] = _sigmoid(p_gate).astype(BF16)
        p_gate = nxt


def _in_projection(xs, mod_l, mod_row_fn, tm, tiles_per_seq, gn, lw, cos_t, sin_t):
    r, d = xs.shape
    n_tiles = r // tm
    row = lambda w: pl.BlockSpec((tm, w), lambda i: (i, 0))
    modspec = lambda chunk: pl.BlockSpec((None, None, 1, d), lambda i: (mod_row_fn(i), chunk, 0, 0))
    pos = pl.BlockSpec((tm, LANES), lambda i: (i % tiles_per_seq, 0))
    in_specs = [
        row(d), modspec(0), modspec(1), _resident((1, d)), _resident((d, C_TOTAL)), _resident((NA_WIDTH, d)),
        _resident((1, NA_WIDTH)), _resident((1, NA_WIDTH)), _resident((2 * LANES, 2 * LANES)),
        _resident((1, MLA_KV_RANK)), _resident((1, MLA_Q_RANK)), _resident((1, GM_WIDTH)), _resident((1, GM_WIDTH)),
        _resident((MLA_Q_RANK, MLA_HEADS * MLA_SLOT)), _resident((MLA_Q_RANK, MLA_HEADS * MLA_SLOT)),
        _resident((1, MLA_HEADS * MLA_SLOT)), _resident((1, MLA_HEADS * MLA_SLOT)),
        _resident((2 * LANES, MLA_HEADS * MLA_SLOT)), _resident((2 * LANES, MLA_HEADS * MLA_SLOT)),
        _resident((1, MLA_HEADS * MLA_SLOT)), _resident((1, MLA_HEADS * MLA_SLOT)),
        _resident((MLA_HEADS * MLA_V, MLA_KV_RANK)), pos, pos,
    ]
    outs = [("naq", NA_WIDTH, False), ("nak", NA_WIDTH, False), ("navt", NA_WIDTH, True), ("gmu", GM_WIDTH, False),
            ("gmv", GM_WIDTH, False), ("gates", N_BRANCHES * d, False), ("qm", MLA_HEADS * MLA_SLOT, False),
            ("km", MLA_HEADS * MLA_SLOT, False), ("vmt", MLA_HEADS * MLA_V, True)]
    col = lambda w: pl.BlockSpec((w, tm), lambda i: (0, i))
    return pl.pallas_call(
        _inproj_kernel,
        grid=(n_tiles,),
        in_specs=in_specs,
        out_specs=[col(w) if t else row(w) for _, w, t in outs],
        out_shape=[jax.ShapeDtypeStruct((w, r) if t else (r, w), BF16) for _, w, t in outs],
        compiler_params=_params(),
        name="in_projection",
    )(xs, mod_l, mod_l, gn, lw["w_in"], lw["w_navt"], lw["na_qg"], lw["na_kg"], lw["bd"], lw["ckv_g"], lw["cq_g"],
      lw["ln_g"],
      lw["ln_b"], lw["wq"], lw["wqp"], lw["gq"], lw["gqp"], lw["wk"], lw["wkp"], lw["gk"], lw["gkp"], lw["wvt"],
      cos_t, sin_t)


def _pair_scores(q_pair, k_blocks):
    lane = lax.broadcasted_iota(jnp.int32, (1, LANES), 1)
    zero = jnp.zeros_like(q_pair)
    q2 = jnp.concatenate([jnp.where(lane < NA_HEAD_DIM, q_pair, zero), jnp.where(lane >= NA_HEAD_DIM, q_pair, zero)],
                         axis=0)
    return [_dot_t(k_pair, q2) if bias_t is None else _dot_t(k_pair, q2) + bias_t for k_pair, bias_t in k_blocks]


def _pair_softmax_pv(sts, vt_blocks):
    nq = sts[0].shape[1] // 2
    m = None
    for st in sts:
        mb = jnp.max(st, axis=0, keepdims=True)
        m = mb if m is None else jnp.maximum(m, mb)
    acc = None
    for st, vt_pair in zip(sts, vt_blocks):
        pt = jnp.exp2(st - m).astype(BF16)
        ones = jnp.ones((BF16_SUBLANES, vt_pair.shape[1]), BF16)
        part = _dot(jnp.concatenate([vt_pair, ones], axis=0), pt)
        acc = part if acc is None else acc + part
    o0 = acc[:NA_HEAD_DIM, :nq] / acc[LANES:LANES + 1, :nq]
    o1 = acc[NA_HEAD_DIM:LANES, nq:] / acc[LANES:LANES + 1, nq:]
    return jnp.concatenate([o0, o1], axis=0).T


def _pairs_pipelined(n_pairs, scores_fn, finish_fn):
    all_scores = [scores_fn(p) for p in range(n_pairs)]
    for p in range(n_pairs):
        finish_fn(p, all_scores[p])


def _na_kernel(q_ref, k_ref, vt_ref, kc_ref, vct_ref, bias_ref, o_ref, *, grid_rows):
    rb = pl.program_id(1)
    n_rb = grid_rows // NA_QR
    cls = jnp.where(rb == 0, 0, jnp.where(rb == n_rb - 1, 2, 1))
    band0 = jnp.clip(NA_QR * rb - NA_WIN_H // 2, 0, grid_rows - NA_BAND)
    k0 = pl.multiple_of(band0 * GRID_W, 2 * LANES)
    nk = NA_BAND * GRID_W
    cols = lambda hp: slice(hp * LANES, (hp + 1) * LANES)

    def scores(hp):
        cs = cols(hp)
        k_all = jnp.concatenate([k_ref[pl.ds(k0, nk), cs], kc_ref[:, cs]], axis=0)
        st = _pair_scores(q_ref[:, cs], [(k_all, None)])[0]
        return [jnp.concatenate([st[:nk] + bias_ref[cls, hp], st[nk:]], axis=0)]

    def finish(hp, sts):
        cs = cols(hp)
        vt_all = jnp.concatenate([vt_ref[cs, pl.ds(k0, nk)], vct_ref[cs, :]], axis=1)
        o_ref[:, cs] = _pair_softmax_pv(sts, [vt_all]).astype(BF16)

    _pairs_pipelined(NA_HEADS // 2, scores, finish)


def _na_attention(q, k, vt, kc, vct, bias, layer, batch, seq, ctx_len):
    grid_rows = seq // GRID_W
    n_rb = grid_rows // NA_QR
    tq = NA_QR * GRID_W
    nk = NA_BAND * GRID_W

    return pl.pallas_call(
        functools.partial(_na_kernel, grid_rows=grid_rows),
        grid=(batch, n_rb),
        in_specs=[
            pl.BlockSpec((tq, NA_WIDTH), lambda b, rb: (b * n_rb + rb, 0)),
            pl.BlockSpec((seq, NA_WIDTH), lambda b, rb: (b, 0)),
            pl.BlockSpec((NA_WIDTH, seq), lambda b, rb: (0, b)),
            pl.BlockSpec((ctx_len, NA_WIDTH), lambda b, rb: (b, 0)),
            pl.BlockSpec((NA_WIDTH, ctx_len), lambda b, rb: (0, b)),
            pl.BlockSpec((None, 3, NA_HEADS // 2, nk, 2 * tq), lambda b, rb: (layer, 0, 0, 0, 0),
                         pipeline_mode=pl.Buffered(1)),
        ],
        out_specs=pl.BlockSpec((tq, NA_WIDTH), lambda b, rb: (b * n_rb + rb, 0)),
        out_shape=jax.ShapeDtypeStruct((batch * seq, NA_WIDTH), BF16),
        compiler_params=_params(),
        name="na_attention",
    )(q, k, vt, kc, vct, bias)


NA_DR = 2 * NA_WIN_H - 1
NA_DC = 2 * NA_WIN_W - 1


def _na_window_geometry(grid_rows):
    n_rb = grid_rows // NA_QR
    kh, kw = NA_WIN_H, NA_WIN_W
    cq = np.arange(GRID_W)
    c0 = np.clip(cq - kw // 2, 0, GRID_W - kw)
    col_in = (cq[None, :] >= c0[:, None]) & (cq[None, :] < c0[:, None] + kw)
    dc = np.clip(cq[None, :] - cq[:, None], -(kw - 1), kw - 1) + kw - 1
    dc_t = np.tile(dc.T, (1, 2)).astype(np.int32)
    col_in_t = np.tile(col_in.T, (1, 2)).astype(np.int32)
    dr = np.zeros((3, NA_QR, NA_BAND), np.int32)
    ok = np.zeros((3, NA_QR, NA_BAND), bool)
    for c, rb in enumerate((0, 1, n_rb - 1)):
        band0 = int(np.clip(NA_QR * rb - kh // 2, 0, grid_rows - NA_BAND))
        for i in range(NA_QR):
            rq = NA_QR * rb + i
            r0 = int(np.clip(rq - kh // 2, 0, grid_rows - kh))
            for j in range(NA_BAND):
                kr = band0 + j
                ok[c, i, j] = r0 <= kr < r0 + kh
                dr[c, i, j] = int(np.clip(kr - rq + kh - 1, 0, NA_DR - 1))
    return dc_t, col_in_t, dr, ok


def _na_bias_kernel(rpb_ref, dc_ref, colin_ref, o_ref, tile_ref, *, dr, ok):
    layer = pl.program_id(0)
    dc = dc_ref[...]
    col_in = colin_ref[...] > 0
    n_tiles = NA_HEADS * NA_DR

    def build(t, carry):
        base = (layer * n_tiles + t) * NA_DC
        acc = jnp.zeros(dc.shape, F32)
        for d in range(NA_DC):
            acc = jnp.where(dc == d, rpb_ref[base + d], acc)
        tile_ref[t] = jnp.where(col_in, acc * LOG2E, NEG_INF)
        return carry

    lax.fori_loop(0, n_tiles, build, 0)

    lane = lax.broadcasted_iota(jnp.int32, (1, LANES), 1)
    neg = jnp.full((GRID_W, LANES), NEG_INF, F32)
    for c in range(3):
        for hp in range(NA_HEADS // 2):
            for j in range(NA_BAND):
                for g in range(2 * NA_QR // 2):
                    head = 2 * hp + g // (NA_QR // 2)
                    i0 = 2 * (g % (NA_QR // 2))
                    halves = [tile_ref[head * NA_DR + int(dr[c, i, j])] if ok[c, i, j] else neg for i in (i0, i0 + 1)]
                    o_ref[c, hp, j * GRID_W:(j + 1) * GRID_W, g * LANES:(g + 1) * LANES] = jnp.where(
                        lane < GRID_W, halves[0], halves[1])


def _na_bias_tables(na_rpb, grid_rows):
    depth = na_rpb.shape[0]
    dc_t, col_in_t, dr, ok = _na_window_geometry(grid_rows)
    nk, nq2 = NA_BAND * GRID_W, 2 * NA_QR * GRID_W
    return pl.pallas_call(
        functools.partial(_na_bias_kernel, dr=dr, ok=ok),
        grid=(depth,),
        in_specs=[
            pl.BlockSpec(memory_space=pltpu.SMEM),
            pl.BlockSpec((GRID_W, LANES), lambda l: (0, 0)),
            pl.BlockSpec((GRID_W, LANES), lambda l: (0, 0)),
        ],
        out_specs=pl.BlockSpec((None, 3, NA_HEADS // 2, nk, nq2), lambda l: (l, 0, 0, 0, 0)),
        out_shape=jax.ShapeDtypeStruct((depth, 3, NA_HEADS // 2, nk, nq2), F32),
        scratch_shapes=[pltpu.VMEM((NA_HEADS * NA_DR, GRID_W, LANES), F32)],
        compiler_params=_params(),
        name="na_bias_tables",
    )(na_rpb.reshape(-1), jnp.asarray(dc_t), jnp.asarray(col_in_t))


def _mla_kernel(q_ref, kc_ref, vct_ref, *rest, n_kb, tk):
    if n_kb:
        k_ref, vt_ref, o_ref = rest
    else:
        (o_ref,) = rest
    tq = q_ref.shape[0]
    heads = (slice(0, MLA_SLOT), slice(MLA_SLOT, 2 * MLA_SLOT))
    q = tuple(q_ref[:, hs] for hs in heads)

    def scores(k_blk):
        return tuple(_dot_t(k_blk[:, heads[h]], q[h]) for h in range(2))

    row = lax.broadcasted_iota(jnp.int32, (2 * MLA_V, 1), 0)
    lo = row < MLA_V

    def values(vt_blk):
        one = jnp.ones_like(vt_blk)
        return jnp.where(lo, vt_blk, one).astype(F32), jnp.where(lo, one, vt_blk).astype(F32)

    def update(st, m, acc, vt_aug):
        m_new = jnp.maximum(m, jnp.max(st, axis=0, keepdims=True))
        return m_new, jnp.exp2(m - m_new) * acc + _dot(vt_aug, jnp.exp2(st - m_new))

    def process(st, state, vt_blk):
        vt_aug = values(vt_blk)
        out = ()
        for h in range(2):
            out += update(st[h], state[2 * h], state[2 * h + 1], vt_aug[h])
        return out

    m0 = jnp.full((1, tq), NEG_INF, F32)
    a0 = jnp.zeros((2 * MLA_V, tq), F32)
    state = (m0, a0, m0, a0)
    s_cur = scores(kc_ref[...])
    if n_kb:
        s_nxt = scores(k_ref[pl.ds(0, tk), :])
        state = process(s_cur, state, vct_ref[...])

        for j in range(n_kb - 1):
            s_cur, s_nxt = s_nxt, scores(k_ref[pl.ds((j + 1) * tk, tk), :])
            state = process(s_cur, state, vt_ref[:, pl.ds(j * tk, tk)])
        state = process(s_nxt, state, vt_ref[:, pl.ds((n_kb - 1) * tk, tk)])
    else:
        state = process(s_cur, state, vct_ref[...])
    o0 = state[1][:MLA_V] / state[1][MLA_V:]
    o1 = state[3][MLA_V:] / state[3][:MLA_V]
    o_ref[...] = jnp.concatenate([o0, o1], axis=0).T.astype(BF16)


def _mla_attention(q, kc, vct, k, vt, batch, q_len, ctx_len, kv_len, tq, tk):
    n_q = q_len // tq
    n_pairs = MLA_HEADS // 2
    n_kb = kv_len // tk if k is not None else 0
    in_specs = [
        pl.BlockSpec((tq, 2 * MLA_SLOT), lambda b, p, i: (b * n_q + i, p)),
        pl.BlockSpec((ctx_len, 2 * MLA_SLOT), lambda b, p, i: (b, p)),
        pl.BlockSpec((2 * MLA_V, ctx_len), lambda b, p, i: (p, b)),
    ]
    args = [q, kc, vct]
    if n_kb:
        in_specs += [
            pl.BlockSpec((kv_len, 2 * MLA_SLOT), lambda b, p, i: (b, p)),
            pl.BlockSpec((2 * MLA_V, kv_len), lambda b, p, i: (p, b)),
        ]
        args += [k, vt]
    return pl.pallas_call(
        functools.partial(_mla_kernel, n_kb=n_kb, tk=tk),
        grid=(batch, n_pairs, n_q),
        in_specs=in_specs,
        out_specs=pl.BlockSpec((tq, LANES), lambda b, p, i: (b * n_q + i, p)),
        out_shape=jax.ShapeDtypeStruct((batch * q_len, MLA_HEADS * MLA_V), BF16),
        compiler_params=_params(),
        name="mla_attention" if n_kb else "mla_ctx_attention",
    )(*args)


def _na_ctx_kernel(q_ref, k_ref, vt_ref, o_ref):
    cols = lambda hp: slice(hp * LANES, (hp + 1) * LANES)

    def scores(hp):
        return _pair_scores(q_ref[:, cols(hp)], [(k_ref[:, cols(hp)], None)])

    def finish(hp, sts):
        o_ref[:, cols(hp)] = _pair_softmax_pv(sts, [vt_ref[cols(hp), :]]).astype(BF16)

    _pairs_pipelined(NA_HEADS // 2, scores, finish)


def _na_ctx_attention(q, k, vt, batch, ctx_len):
    spec = pl.BlockSpec((ctx_len, NA_WIDTH), lambda b: (b, 0))
    return pl.pallas_call(
        _na_ctx_kernel,
        grid=(batch,),
        in_specs=[spec, spec, pl.BlockSpec((NA_WIDTH, ctx_len), lambda b: (0, b))],
        out_specs=spec,
        out_shape=jax.ShapeDtypeStruct((batch * ctx_len, NA_WIDTH), BF16),
        compiler_params=_params(),
        name="na_ctx_attention",
    )(q, k, vt)


FFN_HIDDEN_CHUNK = 512


def _merge_ffn_kernel(x_ref, yna_ref, ymla_ref, gmu_ref, gmv_ref, gates_ref, gt1_ref, sh2_ref, sc2_ref, gt2_ref,
                      gn2_ref, wna_ref, wgm_ref, wmla_ref, wout_ref, w1_ref, w2_ref, ws_ref, bsb_ref, o_ref,
                      ygm_ref):
    tm, d = x_ref.shape
    for c in range(tm // GM_CHUNK):
        rs = slice(c * GM_CHUNK, (c + 1) * GM_CHUNK)
        for g in range(GM_GROUPS):
            cs = slice(g * GM_GROUP_DIM, (g + 1) * GM_GROUP_DIM)
            mixed = _dot(ws_ref[g], gmv_ref[rs, cs]) + bsb_ref[:, cs]
            ygm_ref[rs, cs] = (gmu_ref[rs, cs].astype(F32) * mixed).astype(BF16)

    y = gates_ref[:, 0:d].astype(F32) * _dot(yna_ref[...], wna_ref[...])
    y = y + gates_ref[:, d:2 * d].astype(F32) * _dot(ygm_ref[...], wgm_ref[...])
    y = y + gates_ref[:, 2 * d:3 * d].astype(F32) * _dot(ymla_ref[...], wmla_ref[...])
    xn = x_ref[...] + gt1_ref[...] * _dot(y.astype(BF16), wout_ref[...])

    h2 = _rms(xn, d) * gn2_ref[...]
    h2 = (h2 * (1.0 + sc2_ref[...]) + sh2_ref[...]).astype(BF16)
    hidden = w1_ref.shape[1]
    acc = None
    for c in range(hidden // FFN_HIDDEN_CHUNK):
        cs = slice(c * FFN_HIDDEN_CHUNK, (c + 1) * FFN_HIDDEN_CHUNK)
        a = jnp.maximum(_dot(h2, w1_ref[:, cs]), 0.0)
        part = _dot((a * a).astype(BF16), w2_ref[cs, :])
        acc = part if acc is None else acc + part
    o_ref[...] = xn + gt2_ref[...] * acc


def _merge_ffn(xs, y_na, y_mla, gm_u, gm_v, gates, mod_l, mod_row_fn, tm, gn2, lw):
    r, d = xs.shape
    hidden = lw["w1"].shape[1]
    row = lambda w: pl.BlockSpec((tm, w), lambda i: (i, 0))
    modspec = lambda chunk: pl.BlockSpec((None, None, 1, d), lambda i: (mod_row_fn(i), chunk, 0, 0))
    in_specs = [
        row(d), row(NA_WIDTH), row(MLA_HEADS * MLA_V), row(GM_WIDTH), row(GM_WIDTH), row(N_BRANCHES * d),
        modspec(2), modspec(3), modspec(4), modspec(5), _resident((1, d)),
        _resident((NA_WIDTH, d)), _resident((GM_WIDTH, d)), _resident((MLA_HEADS * MLA_V, d)), _resident((d, d)),
        _resident((d, hidden)), _resident((hidden, d)), _resident((GM_GROUPS, GM_CHUNK, GM_CHUNK)),
        _resident((GM_CHUNK, GM_WIDTH)),
    ]
    return pl.pallas_call(
        _merge_ffn_kernel,
        grid=(r // tm,),
        in_specs=in_specs,
        out_specs=row(d),
        out_shape=jax.ShapeDtypeStruct((r, d), F32),
        scratch_shapes=[pltpu.VMEM((tm, GM_WIDTH), BF16)],
        compiler_params=_params(),
        name="merge_ffn",
    )(xs, y_na, y_mla, gm_u, gm_v, gates, mod_l, mod_l, mod_l, mod_l, gn2, lw["w_na_o"], lw["w_gm_o"],
      lw["w_mla_o"], lw["w_out"], lw["w1"], lw["w2"], lw["w_s"], lw["b_s"])


def _rope_partner():
    perm = np.arange(MLA_QK)
    for u in range(MLA_ROPE):
        seg, half, pair = u // (2 * ROPE_AXIS_PAIRS), (u // ROPE_AXIS_PAIRS) % 2, u % ROPE_AXIS_PAIRS
        perm[MLA_NOPE + u] = MLA_NOPE + seg * 2 * ROPE_AXIS_PAIRS + (1 - half) * ROPE_AXIS_PAIRS + pair
    tail = np.arange(MLA_QK) >= MLA_NOPE
    return perm, tail


def _pad_heads(t):
    pad = [(0, 0)] * (t.ndim - 1) + [(0, MLA_SLOT - MLA_QK)]
    t = jnp.pad(t, pad)
    return t.reshape(t.shape[:-2] + (MLA_HEADS * MLA_SLOT,))


def _rope_tables(seq):
    t = jnp.arange(seq)
    rows = (t // GRID_W).astype(F32)
    colsv = (t % GRID_W).astype(F32)
    freqs = ROPE_THETA ** (-jnp.arange(ROPE_AXIS_PAIRS, dtype=F32) / ROPE_AXIS_PAIRS)
    ang = jnp.stack([rows[:, None] * freqs, colsv[:, None] * freqs], axis=1)
    cos, sin = jnp.cos(ang), jnp.sin(ang)
    cos_t = jnp.stack([cos, cos], axis=2).reshape(seq, MLA_ROPE)
    sin_t = jnp.stack([-sin, sin], axis=2).reshape(seq, MLA_ROPE)
    ones = jnp.ones((seq, MLA_NOPE), F32)
    padc = jnp.ones((seq, MLA_SLOT - MLA_QK), F32)
    cos_full = jnp.concatenate([ones, cos_t, padc], axis=1)
    sin_full = jnp.concatenate([0 * ones, sin_t, 0 * padc], axis=1)
    return cos_full, sin_full


def _pack_layer(i, p, off):
    d = p["w_in"].shape[1]
    w = p["w_in"][i]
    kr = w[:, off["kr"]:off["kr"] + MLA_ROPE]
    w_in = jnp.concatenate([
        w[:, off["naq"]:off["naq"] + NA_WIDTH], w[:, off["nak"]:off["nak"] + NA_WIDTH],
        w[:, off["ckv"]:off["ckv"] + MLA_KV_RANK],
        kr, kr, jnp.zeros((d, LANES - 2 * MLA_ROPE), w.dtype),
        w[:, off["cq"]:off["cq"] + MLA_Q_RANK], w[:, off["gmu"]:off["gmu"] + GM_WIDTH],
        w[:, off["gmv"]:off["gmv"] + GM_WIDTH], w[:, off["gates"]:off["gates"] + N_BRANCHES * d],
    ], axis=1).astype(BF16)

    perm, tail = _rope_partner()
    q_scale = MLA_QK ** -0.5 * LOG2E
    w_uq = p["mla_w_uq"][i].reshape(MLA_Q_RANK, MLA_HEADS, MLA_QK)
    wq = _pad_heads(w_uq).astype(BF16)
    wqp = _pad_heads(w_uq[:, :, perm] * tail).astype(BF16)
    qg = p["mla_q_gain"][i]
    gq = jnp.tile(jnp.pad(qg, (0, MLA_SLOT - MLA_QK)), MLA_HEADS)[None] * q_scale
    gqp = jnp.tile(jnp.pad(qg[perm] * tail, (0, MLA_SLOT - MLA_QK)), MLA_HEADS)[None] * q_scale

    w_ukv = p["mla_w_ukv"][i].reshape(MLA_KV_RANK, MLA_HEADS, MLA_NOPE + MLA_V)
    k_nope = jnp.pad(w_ukv[:, :, :MLA_NOPE], ((0, 0), (0, 0), (0, MLA_SLOT - MLA_NOPE)))
    k_nope = k_nope.reshape(MLA_KV_RANK, MLA_HEADS * MLA_SLOT)
    sel = np.zeros((LANES, MLA_HEADS, MLA_SLOT), np.float32)
    selp = np.zeros((LANES, MLA_HEADS, MLA_SLOT), np.float32)
    for t in range(MLA_ROPE):
        for copy in range(2):
            sel[copy * MLA_ROPE + t, :, MLA_NOPE + t] = 1.0
            selp[copy * MLA_ROPE + t, :, perm[MLA_NOPE + t]] = 1.0
    wk = jnp.concatenate([k_nope, sel.reshape(LANES, -1)], axis=0).astype(BF16)
    wkp = jnp.concatenate([jnp.zeros_like(k_nope), selp.reshape(LANES, -1)], axis=0).astype(BF16)
    kg = p["mla_k_gain"][i]
    gk = jnp.tile(jnp.pad(kg, (0, MLA_SLOT - MLA_QK)), MLA_HEADS)[None]
    gkp = jnp.tile(jnp.pad(kg[perm] * tail, (0, MLA_SLOT - MLA_QK)), MLA_HEADS)[None]
    wvt = w_ukv[:, :, MLA_NOPE:].reshape(MLA_KV_RANK, MLA_HEADS * MLA_V).T.astype(BF16)

    blk = np.arange(2 * LANES) // NA_HEAD_DIM
    bd = jnp.asarray(blk[:, None] == blk[None, :], BF16)

    return {
        "w_in": w_in,
        "w_navt": w[:, off["nav"]:off["nav"] + NA_WIDTH].T.astype(BF16),
        "na_qg": jnp.tile(p["na_q_gain"][i], NA_HEADS)[None] * (NA_HEAD_DIM ** -0.5 * LOG2E),
        "na_kg": jnp.tile(p["na_k_gain"][i], NA_HEADS)[None],
        "bd": bd,
        "ckv_g": p["mla_ckv_gain"][i][None], "cq_g": p["mla_cq_gain"][i][None],
        "ln_g": p["gm_ln_g"][i][None], "ln_b": p["gm_ln_b"][i][None],
        "wq": wq, "wqp": wqp, "gq": gq, "gqp": gqp, "wk": wk, "wkp": wkp, "gk": gk, "gkp": gkp, "wvt": wvt,
        "w_na_o": p["na_w_o"][i].astype(BF16), "w_gm_o": p["gm_w_o"][i].astype(BF16),
        "w_mla_o": p["mla_w_o"][i].astype(BF16), "w_out": p["w_out"][i].astype(BF16),
        "w1": p["ffn_w1"][i].astype(BF16), "w2": p["ffn_w2"][i].astype(BF16),
        "w_s": p["gm_w_s"][i].astype(BF16),
        "b_s": jnp.repeat(p["gm_b_s"][i].T, GM_GROUP_DIM, axis=1),
    }


def _in_offsets(d):
    off = {"nak": 0}
    off["nav"] = off["nak"] + NA_WIDTH
    off["ckv"] = off["nav"] + NA_WIDTH
    off["kr"] = off["ckv"] + MLA_KV_RANK
    off["naq"] = off["kr"] + MLA_ROPE
    off["gmu"] = off["naq"] + NA_WIDTH
    off["gmv"] = off["gmu"] + GM_WIDTH
    off["cq"] = off["gmv"] + GM_WIDTH
    off["gates"] = off["cq"] + MLA_Q_RANK
    return off


def kernel(x, c, ctx, c_ctx, w_mod, b_mod, g_norm1, g_norm2, w_in, na_q_gain, na_k_gain, na_rpb, na_w_o, gm_ln_g,
           gm_ln_b, gm_w_s, gm_b_s, gm_w_o, mla_cq_gain, mla_ckv_gain, mla_w_uq, mla_w_ukv, mla_q_gain, mla_k_gain,
           mla_w_o, w_out, ffn_w1, ffn_w2):
    batch, seq, d = x.shape
    ctx_len = ctx.shape[1]
    depth = w_mod.shape[0]
    p = dict(w_in=w_in.astype(BF16), na_q_gain=na_q_gain, na_k_gain=na_k_gain, na_w_o=na_w_o, gm_ln_g=gm_ln_g, gm_ln_b=gm_ln_b,
             gm_w_s=gm_w_s, gm_b_s=gm_b_s, gm_w_o=gm_w_o, mla_cq_gain=mla_cq_gain, mla_ckv_gain=mla_ckv_gain,
             mla_w_uq=mla_w_uq, mla_w_ukv=mla_w_ukv, mla_q_gain=mla_q_gain, mla_k_gain=mla_k_gain, mla_w_o=mla_w_o,
             w_out=w_out, ffn_w1=ffn_w1, ffn_w2=ffn_w2)
    off = _in_offsets(d)

    mod_rows = 16
    ctx_row = batch
    rows = jnp.concatenate([c, c_ctx[None], jnp.zeros((mod_rows - batch - 1, d), F32)], axis=0)
    mod = _modulation(rows, w_mod, b_mod).reshape(depth, mod_rows, 6, 1, d)

    cos_t, sin_t = _rope_tables(seq)
    cos_c = jnp.ones((ctx_len, LANES), F32)
    sin_c = jnp.zeros((ctx_len, LANES), F32)

    tm = min(ROW_TILE, seq)
    tmc = min(CTX_ROW_TILE, ctx_len)
    tiles_per_seq = seq // tm
    tiles_per_ctx = ctx_len // tmc
    lat_row = lambda i: i // tiles_per_seq
    ctx_row_fn = lambda i: ctx_row
    grid_rows = seq // GRID_W
    na_bias = _na_bias_tables(na_rpb, grid_rows)

    xs = x.reshape(batch * seq, d)
    cs = ctx.reshape(batch * ctx_len, d)
    for i in range(depth):
        last = i == depth - 1
        lw = _pack_layer(i, p, off)
        mod_l = mod[i]
        gn1 = g_norm1[i][None]
        gn2 = g_norm2[i][None]

        (naq, nak, navt, gmu, gmv, gates, qm, km, vmt) = _in_projection(
            xs, mod_l, lat_row, tm, tiles_per_seq, gn1, lw, cos_t, sin_t)
        (naq_c, nak_c, navt_c, gmu_c, gmv_c, gates_c, qm_c, km_c, vmt_c) = _in_projection(
            cs, mod_l, ctx_row_fn, tmc, tiles_per_ctx, gn1, lw, cos_c, sin_c)

        y_na = _na_attention(naq, nak, navt, nak_c, navt_c, na_bias, i, batch, seq, ctx_len)
        y_mla = _mla_attention(qm, km_c, vmt_c, km, vmt, batch, seq, ctx_len, seq, tq=min(MLA_TQ, seq),
                               tk=min(MLA_TK, seq))
        xs_new = _merge_ffn(xs, y_na, y_mla, gmu, gmv, gates, mod_l, lat_row, tm, gn2, lw)

        if not last:
            y_na_c = _na_ctx_attention(naq_c, nak_c, navt_c, batch, ctx_len)
            y_mla_c = _mla_attention(qm_c, km_c, vmt_c, None, None, batch, ctx_len, ctx_len, 0, tq=ctx_len, tk=0)
            cs = _merge_ffn(cs, y_na_c, y_mla_c, gmu_c, gmv_c, gates_c, mod_l, ctx_row_fn, tmc, gn2, lw)
        xs = xs_new
    return xs.reshape(batch, seq, d)
```

```python
import functools
import math

import numpy as np
import jax
import jax.numpy as jnp
from jax import lax
from jax.experimental import pallas as pl
from jax.experimental.pallas import tpu as pltpu

F32 = jnp.float32
BF16 = jnp.bfloat16

LANES = 128
BF16_SUBLANES = 16
VMEM_LIMIT_BYTES = 60 * 1024 * 1024

GRID_W = 64
N_BRANCHES = 3
NA_HEADS = 8
NA_HEAD_DIM = 64
NA_WIDTH = NA_HEADS * NA_HEAD_DIM
NA_WIN_H = 8
NA_WIN_W = 16
GM_GROUPS = 4
GM_CHUNK = 128
GM_GROUP_DIM = 128
GM_WIDTH = GM_GROUPS * GM_GROUP_DIM
MLA_HEADS = 8
MLA_Q_RANK = 256
MLA_KV_RANK = 128
MLA_NOPE = 64
MLA_ROPE = 32
MLA_V = 64
MLA_QK = MLA_NOPE + MLA_ROPE
ROPE_AXIS_PAIRS = MLA_ROPE // 4
ROPE_THETA = 10000.0
NORM_EPS = 1e-6
NEG_INF = -1e30
LOG2E = math.log2(math.e)

NA_QR = 4
NA_BAND = 12
MLA_SLOT = LANES
ROW_TILE = 512
CTX_ROW_TILE = 512
MLA_TQ = 2048
MLA_TK = 256
MLA_CHAIN_Q = 256


def _resident(shape):
    zeros = (0,) * len(shape)
    return pl.BlockSpec(shape, lambda *_: zeros, pipeline_mode=pl.Buffered(1))


def _resident_layer(shape, layer):
    zeros = (0,) * len(shape)
    return pl.BlockSpec((None,) + tuple(shape), lambda *_: (layer,) + zeros, pipeline_mode=pl.Buffered(1))


def _params():
    return pltpu.CompilerParams(vmem_limit_bytes=VMEM_LIMIT_BYTES)


def _dot(a, b):
    return jnp.dot(a, b, preferred_element_type=F32)


def _dot_t(a, b):
    return lax.dot_general(a, b, (((1,), (1,)), ((), ())), preferred_element_type=F32)


def _mod_kernel(s_ref, w_ref, b_ref, o_ref):
    s = s_ref[...]
    s = s * jax.nn.sigmoid(s)
    o_ref[...] = _dot(s.astype(BF16), w_ref[...].astype(BF16)) + b_ref[...]


def _modulation(rows, w_mod, b_mod):
    depth, d, n = w_mod.shape
    r = rows.shape[0]
    tn = 1536
    return pl.pallas_call(
        _mod_kernel,
        grid=(depth, n // tn),
        in_specs=[
            pl.BlockSpec((r, d), lambda l, j: (0, 0)),
            pl.BlockSpec((None, d, tn), lambda l, j: (l, 0, j)),
            pl.BlockSpec((None, 1, tn), lambda l, j: (l, 0, j)),
        ],
        out_specs=pl.BlockSpec((None, r, tn), lambda l, j: (l, 0, j)),
        out_shape=jax.ShapeDtypeStruct((depth, r, n), F32),
        compiler_params=_params(),
        name="modulation",
    )(rows, w_mod, b_mod.reshape(depth, 1, n))


C_NAQ = 0
C_NAK = C_NAQ + NA_WIDTH
C_CKV = C_NAK + NA_WIDTH
C_KR = C_CKV + MLA_KV_RANK
C_CQ = C_KR + LANES
C_GMU = C_CQ + MLA_Q_RANK
C_GMV = C_GMU + GM_WIDTH
C_GATES = C_GMV + GM_WIDTH
C_TOTAL = C_GATES + N_BRANCHES * 1024


def _rms(t, width):
    return t * lax.rsqrt(jnp.sum(t * t, axis=-1, keepdims=True) * (1.0 / width) + NORM_EPS)


def _gelu(t):
    return 0.5 * t * (1.0 + lax.erf(t * math.sqrt(0.5)))


def _sigmoid(t):
    return 0.5 * jnp.tanh(0.5 * t) + 0.5


def _head64_norm(p, bd):
    p2 = (p * p).astype(BF16)
    half = 2 * LANES
    ss = jnp.concatenate([_dot(p2[:, :half], bd), _dot(p2[:, half:], bd)], axis=1)
    return p * lax.rsqrt(ss * (1.0 / NA_HEAD_DIM) + NORM_EPS)


def _inproj_kernel(x_ref, sh_ref, sc_ref, gn_ref, w_ref, wnavt_ref, qg_ref, kg_ref, bd_ref, ckvg_ref, cqg_ref, lng_ref,
                   lnb_ref, wq_ref, wqp_ref, gq_ref, gqp_ref, wk_ref, wkp_ref, gk_ref, gkp_ref, wvt_ref, cos_ref,
                   sin_ref, naq_ref, nak_ref, navt_ref, gmu_ref, gmv_ref, gates_ref, qm_ref, km_ref, vmt_ref):
    x = x_ref[...]
    d = x.shape[-1]
    h = _rms(x, d) * gn_ref[...]
    h = h * (1.0 + sc_ref[...]) + sh_ref[...]
    hb = h.astype(BF16)

    def proj(c0, width):
        return _dot(hb, w_ref[:, c0:c0 + width])

    p_ckv = proj(C_CKV, MLA_KV_RANK)
    p_cq = proj(C_CQ, MLA_Q_RANK)
    kr = proj(C_KR, LANES)
    p_naq = proj(C_NAQ, NA_WIDTH)
    p_nak = proj(C_NAK, NA_WIDTH)
    navt_ref[...] = _dot_t(wnavt_ref[...], hb).astype(BF16)

    ckv_n = (_rms(p_ckv, MLA_KV_RANK) * ckvg_ref[...]).astype(BF16)
    cq_n = (_rms(p_cq, MLA_Q_RANK) * cqg_ref[...]).astype(BF16)
    kr_hi = kr.astype(BF16)
    kr_lo = (kr - kr_hi.astype(F32)).astype(BF16)
    lane = lax.broadcasted_iota(jnp.int32, (1, LANES), 1)
    kr_hl = jnp.where(lane < MLA_ROPE, kr_hi, kr_lo)
    lhs_k = jnp.concatenate([ckv_n, kr_hl], axis=1)

    p_gmu = proj(C_GMU, GM_WIDTH)
    p_gmv = proj(C_GMV, GM_WIDTH)

    bd = bd_ref[...]
    naq_ref[...] = (_head64_norm(p_naq, bd) * qg_ref[...]).astype(BF16)
    nak_ref[...] = (_head64_norm(p_nak, bd) * kg_ref[...]).astype(BF16)

    vmt_ref[...] = _dot_t(wvt_ref[...], ckv_n).astype(BF16)
    xq, xqp = _dot(cq_n, wq_ref[...]), _dot(cq_n, wqp_ref[...])
    xk, xkp = _dot(lhs_k, wk_ref[...]), _dot(lhs_k, wkp_ref[...])

    gmu_ref[...] = _gelu(p_gmu).astype(BF16)
    gv = _gelu(p_gmv)
    mu = jnp.mean(gv, axis=-1, keepdims=True)
    gc = gv - mu
    var = jnp.mean(gc * gc, axis=-1, keepdims=True)
    gmv_ref[...] = (gc * lax.rsqrt(var + NORM_EPS) * lng_ref[...] + lnb_ref[...]).astype(BF16)

    n_gate = gates_ref.shape[-1]
    gate_chunk = 512
    p_gate = proj(C_GATES, gate_chunk)

    cosv = cos_ref[...]
    sinv = sin_ref[...]

    def heads(xa, xpa, g_r, gp_r, out_r):
        gcos, gsin = g_r[...] * cosv, gp_r[...] * sinv
        for hd in range(MLA_HEADS):
            sl = slice(hd * MLA_SLOT, (hd + 1) * MLA_SLOT)
            xh = xa[:, sl]
            r = lax.rsqrt(jnp.sum(xh * xh, axis=-1, keepdims=True) * (1.0 / MLA_QK) + NORM_EPS)
            out_r[:, sl] = (r * (xh * gcos + xpa[:, sl] * gsin)).astype(BF16)

    heads(xq, xqp, gq_ref, gqp_ref, qm_ref)
    heads(xk, xkp, gk_ref, gkp_ref, km_ref)

    for j in range(n_gate // gate_chunk):
        nxt = proj(C_GATES + (j + 1) * gate_chunk, gate_chunk) if (j + 1) * gate_chunk < n_gate else None
        gates_ref[:, j * gate_chunk:(j + 1) * gate_chunk] = _sigmoid(p_gate).astype(BF16)
        p_gate = nxt


def _in_projection(xs, mod_l, mod_row_fn, tm, tiles_per_seq, gn, lw, cos_t, sin_t):
    r, d = xs.shape
    n_tiles = r // tm
    row = lambda w: pl.BlockSpec((tm, w), lambda i: (i, 0))
    modspec = lambda chunk: pl.BlockSpec((None, None, 1, d), lambda i: (mod_row_fn(i), chunk, 0, 0))
    pos = pl.BlockSpec((tm, LANES), lambda i: (i % tiles_per_seq, 0))
    in_specs = [
        row(d), modspec(0), modspec(1), _resident((1, d)), _resident_layer((d, C_TOTAL), lw["layer"]),
        _resident((NA_WIDTH, d)),
        _resident((1, NA_WIDTH)), _resident((1, NA_WIDTH)), _resident((2 * LANES, 2 * LANES)),
        _resident((1, MLA_KV_RANK)), _resident((1, MLA_Q_RANK)), _resident((1, GM_WIDTH)), _resident((1, GM_WIDTH)),
        _resident((MLA_Q_RANK, MLA_HEADS * MLA_SLOT)), _resident((MLA_Q_RANK, MLA_HEADS * MLA_SLOT)),
        _resident((1, MLA_SLOT)), _resident((1, MLA_SLOT)),
        _resident((2 * LANES, MLA_HEADS * MLA_SLOT)), _resident((2 * LANES, MLA_HEADS * MLA_SLOT)),
        _resident((1, MLA_SLOT)), _resident((1, MLA_SLOT)),
        _resident((MLA_HEADS * MLA_V, MLA_KV_RANK)), pos, pos,
    ]
    outs = [("naq", NA_WIDTH, False), ("nak", NA_WIDTH, False), ("navt", NA_WIDTH, True), ("gmu", GM_WIDTH, False),
            ("gmv", GM_WIDTH, False), ("gates", N_BRANCHES * d, False), ("qm", MLA_HEADS * MLA_SLOT, False),
            ("km", MLA_HEADS * MLA_SLOT, False), ("vmt", MLA_HEADS * MLA_V, True)]
    col = lambda w: pl.BlockSpec((w, tm), lambda i: (0, i))
    return pl.pallas_call(
        _inproj_kernel,
        grid=(n_tiles,),
        in_specs=in_specs,
        out_specs=[col(w) if t else row(w) for _, w, t in outs],
        out_shape=[jax.ShapeDtypeStruct((w, r) if t else (r, w), BF16) for _, w, t in outs],
        compiler_params=_params(),
        name="in_projection",
    )(xs, mod_l, mod_l, gn, lw["w_in"], lw["w_navt"], lw["na_qg"], lw["na_kg"], lw["bd"], lw["ckv_g"], lw["cq_g"],
      lw["ln_g"],
      lw["ln_b"], lw["wq"], lw["wqp"], lw["gq"], lw["gqp"], lw["wk"], lw["wkp"], lw["gk"], lw["gkp"], lw["wvt"],
      cos_t, sin_t)


def _pair_scores(q_pair, k_blocks):
    lane = lax.broadcasted_iota(jnp.int32, (1, LANES), 1)
    zero = jnp.zeros_like(q_pair)
    q2 = jnp.concatenate([jnp.where(lane < NA_HEAD_DIM, q_pair, zero), jnp.where(lane >= NA_HEAD_DIM, q_pair, zero)],
                         axis=0)
    return [_dot_t(k_pair, q2) if bias_t is None else _dot_t(k_pair, q2) + bias_t for k_pair, bias_t in k_blocks]


def _pair_softmax_pv(sts, vt_blocks):
    nq = sts[0].shape[1] // 2
    m = None
    for st in sts:
        mb = jnp.max(st, axis=0, keepdims=True)
        m = mb if m is None else jnp.maximum(m, mb)
    acc = None
    for st, vt_pair in zip(sts, vt_blocks):
        ones = jnp.ones((BF16_SUBLANES, vt_pair.shape[1]), BF16)
        part = _dot(jnp.concatenate([vt_pair, ones], axis=0), jnp.exp2(st - m).astype(BF16))
        acc = part if acc is None else acc + part
    o0 = acc[:NA_HEAD_DIM, :nq] / acc[LANES:LANES + 1, :nq]
    o1 = acc[NA_HEAD_DIM:LANES, nq:] / acc[LANES:LANES + 1, nq:]
    return jnp.concatenate([o0, o1], axis=0).T


def _pairs_pipelined(n_pairs, scores_fn, finish_fn):
    all_scores = [scores_fn(p) for p in range(n_pairs)]
    for p in range(n_pairs):
        finish_fn(p, all_scores[p])


def _na_kernel(q_ref, k_ref, vt_ref, kc_ref, vct_ref, bias_ref, o_ref, *, grid_rows):
    rb = pl.program_id(1)
    n_rb = grid_rows // NA_QR
    cls = jnp.where(rb == 0, 0, jnp.where(rb == n_rb - 1, 2, 1))
    band0 = jnp.clip(NA_QR * rb - NA_WIN_H // 2, 0, grid_rows - NA_BAND)
    k0 = pl.multiple_of(band0 * GRID_W, 2 * LANES)
    nk = NA_BAND * GRID_W
    cols = lambda hp: slice(hp * LANES, (hp + 1) * LANES)

    def scores(hp):
        cs = cols(hp)
        k_all = jnp.concatenate([k_ref[pl.ds(k0, nk), cs], kc_ref[:, cs]], axis=0)
        st = _pair_scores(q_ref[:, cs], [(k_all, None)])[0]
        return [jnp.concatenate([st[:nk] + bias_ref[cls, hp], st[nk:]], axis=0)]

    def finish(hp, sts):
        cs = cols(hp)
        vt_all = jnp.concatenate([vt_ref[cs, pl.ds(k0, nk)], vct_ref[cs, :]], axis=1)
        o_ref[:, cs] = _pair_softmax_pv(sts, [vt_all]).astype(BF16)

    _pairs_pipelined(NA_HEADS // 2, scores, finish)


def _na_attention(q, k, vt, kc, vct, bias, layer, batch, seq, ctx_len):
    grid_rows = seq // GRID_W
    n_rb = grid_rows // NA_QR
    tq = NA_QR * GRID_W
    nk = NA_BAND * GRID_W

    return pl.pallas_call(
        functools.partial(_na_kernel, grid_rows=grid_rows),
        grid=(batch, n_rb),
        in_specs=[
            pl.BlockSpec((tq, NA_WIDTH), lambda b, rb: (b * n_rb + rb, 0)),
            pl.BlockSpec((seq, NA_WIDTH), lambda b, rb: (b, 0)),
            pl.BlockSpec((NA_WIDTH, seq), lambda b, rb: (0, b)),
            pl.BlockSpec((ctx_len, NA_WIDTH), lambda b, rb: (b, 0)),
            pl.BlockSpec((NA_WIDTH, ctx_len), lambda b, rb: (0, b)),
            pl.BlockSpec((None, 3, NA_HEADS // 2, nk, 2 * tq), lambda b, rb: (layer, 0, 0, 0, 0),
                         pipeline_mode=pl.Buffered(1)),
        ],
        out_specs=pl.BlockSpec((tq, NA_WIDTH), lambda b, rb: (b * n_rb + rb, 0)),
        out_shape=jax.ShapeDtypeStruct((batch * seq, NA_WIDTH), BF16),
        compiler_params=_params(),
        name="na_attention",
    )(q, k, vt, kc, vct, bias)


NA_DR = 2 * NA_WIN_H - 1
NA_DC = 2 * NA_WIN_W - 1


def _na_window_geometry(grid_rows):
    n_rb = grid_rows // NA_QR
    kh, kw = NA_WIN_H, NA_WIN_W
    cq = np.arange(GRID_W)
    c0 = np.clip(cq - kw // 2, 0, GRID_W - kw)
    col_in = (cq[None, :] >= c0[:, None]) & (cq[None, :] < c0[:, None] + kw)
    dc = np.clip(cq[None, :] - cq[:, None], -(kw - 1), kw - 1) + kw - 1
    dc_t = np.tile(dc.T, (1, 2)).astype(np.int32)
    col_in_t = np.tile(col_in.T, (1, 2)).astype(np.int32)
    dr = np.zeros((3, NA_QR, NA_BAND), np.int32)
    ok = np.zeros((3, NA_QR, NA_BAND), bool)
    for c, rb in enumerate((0, 1, n_rb - 1)):
        band0 = int(np.clip(NA_QR * rb - kh // 2, 0, grid_rows - NA_BAND))
        for i in range(NA_QR):
            rq = NA_QR * rb + i
            r0 = int(np.clip(rq - kh // 2, 0, grid_rows - kh))
            for j in range(NA_BAND):
                kr = band0 + j
                ok[c, i, j] = r0 <= kr < r0 + kh
                dr[c, i, j] = int(np.clip(kr - rq + kh - 1, 0, NA_DR - 1))
    return dc_t, col_in_t, dr, ok


def _na_bias_kernel(rpb_ref, dc_ref, colin_ref, o_ref, tile_ref, *, dr, ok):
    layer = pl.program_id(0)
    dc = dc_ref[...]
    col_in = colin_ref[...] > 0
    n_tiles = NA_HEADS * NA_DR

    def build(t, carry):
        base = (layer * n_tiles + t) * NA_DC
        acc = jnp.zeros(dc.shape, F32)
        for d in range(NA_DC):
            acc = jnp.where(dc == d, rpb_ref[base + d], acc)
        tile_ref[t] = jnp.where(col_in, acc * LOG2E, NEG_INF)
        return carry

    lax.fori_loop(0, n_tiles, build, 0)

    lane = lax.broadcasted_iota(jnp.int32, (1, LANES), 1)
    neg = jnp.full((GRID_W, LANES), NEG_INF, F32)
    for c in range(3):
        for hp in range(NA_HEADS // 2):
            for j in range(NA_BAND):
                for g in range(2 * NA_QR // 2):
                    head = 2 * hp + g // (NA_QR // 2)
                    i0 = 2 * (g % (NA_QR // 2))
                    halves = [tile_ref[head * NA_DR + int(dr[c, i, j])] if ok[c, i, j] else neg for i in (i0, i0 + 1)]
                    o_ref[c, hp, j * GRID_W:(j + 1) * GRID_W, g * LANES:(g + 1) * LANES] = jnp.where(
                        lane < GRID_W, halves[0], halves[1])


def _na_bias_tables(na_rpb, grid_rows):
    depth = na_rpb.shape[0]
    dc_t, col_in_t, dr, ok = _na_window_geometry(grid_rows)
    nk, nq2 = NA_BAND * GRID_W, 2 * NA_QR * GRID_W
    return pl.pallas_call(
        functools.partial(_na_bias_kernel, dr=dr, ok=ok),
        grid=(depth,),
        in_specs=[
            pl.BlockSpec(memory_space=pltpu.SMEM),
            pl.BlockSpec((GRID_W, LANES), lambda l: (0, 0)),
            pl.BlockSpec((GRID_W, LANES), lambda l: (0, 0)),
        ],
        out_specs=pl.BlockSpec((None, 3, NA_HEADS // 2, nk, nq2), lambda l: (l, 0, 0, 0, 0)),
        out_shape=jax.ShapeDtypeStruct((depth, 3, NA_HEADS // 2, nk, nq2), F32),
        scratch_shapes=[pltpu.VMEM((NA_HEADS * NA_DR, GRID_W, LANES), F32)],
        compiler_params=_params(),
        name="na_bias_tables",
    )(na_rpb.reshape(-1), jnp.asarray(dc_t), jnp.asarray(col_in_t))


def _mla_kernel(q_ref, kc_ref, vct_ref, *rest, n_kb, tk):
    if n_kb:
        k_ref, vt_ref, o_ref = rest
    else:
        (o_ref,) = rest
    tq = q_ref.shape[0]
    heads = (slice(0, MLA_SLOT), slice(MLA_SLOT, 2 * MLA_SLOT))
    lo = lax.broadcasted_iota(jnp.int32, (2 * MLA_V, 1), 0) < MLA_V

    def values(vt_blk):
        one = jnp.ones_like(vt_blk)
        return jnp.where(lo, vt_blk, one).astype(F32), jnp.where(lo, one, vt_blk).astype(F32)

    def update(st, m, acc, vt_aug):
        m_new = jnp.maximum(m, jnp.max(st, axis=0, keepdims=True))
        return m_new, jnp.exp2(m - m_new) * acc + _dot(vt_aug, jnp.exp2(st - m_new))

    k_blk = lambda j: kc_ref[...] if j < 0 else k_ref[pl.ds(j * tk, tk), :]
    vt_blk = lambda j: vct_ref[...] if j < 0 else vt_ref[:, pl.ds(j * tk, tk)]
    blocks = list(range(-1, n_kb))
    nq = min(MLA_CHAIN_Q, tq)
    chains = [(h, s) for s in range(tq // nq) for h in range(2)]
    q = [q_ref[s * nq:(s + 1) * nq, heads[h]] for h, s in chains]
    scores = lambda j, c: _dot_t(k_blk(j)[:, heads[chains[c][0]]], q[c])
    nxt = [scores(blocks[0], c) for c in range(len(chains))]
    m = [jnp.full((1, nq), NEG_INF, F32)] * len(chains)
    acc = [jnp.zeros((2 * MLA_V, nq), F32)] * len(chains)
    for idx, j in enumerate(blocks):
        vt_aug = values(vt_blk(j))
        for c, (h, _) in enumerate(chains):
            st = nxt[c]
            if idx + 1 < len(blocks):
                nxt[c] = scores(blocks[idx + 1], c)
            m[c], acc[c] = update(st, m[c], acc[c], vt_aug[h])
    acc_h = [jnp.concatenate([acc[c] for c, (h, _) in enumerate(chains) if h == hh], axis=1) for hh in range(2)]
    o0 = acc_h[0][:MLA_V] / acc_h[0][MLA_V:]
    o1 = acc_h[1][MLA_V:] / acc_h[1][:MLA_V]
    o_ref[...] = jnp.concatenate([o0, o1], axis=0).T.astype(BF16)


def _mla_attention(q, kc, vct, k, vt, batch, q_len, ctx_len, kv_len, tq, tk):
    n_q = q_len // tq
    n_pairs = MLA_HEADS // 2
    n_kb = kv_len // tk if k is not None else 0
    in_specs = [
        pl.BlockSpec((tq, 2 * MLA_SLOT), lambda b, p, i: (b * n_q + i, p)),
        pl.BlockSpec((ctx_len, 2 * MLA_SLOT), lambda b, p, i: (b, p)),
        pl.BlockSpec((2 * MLA_V, ctx_len), lambda b, p, i: (p, b)),
    ]
    args = [q, kc, vct]
    if n_kb:
        in_specs += [
            pl.BlockSpec((kv_len, 2 * MLA_SLOT), lambda b, p, i: (b, p)),
            pl.BlockSpec((2 * MLA_V, kv_len), lambda b, p, i: (p, b)),
        ]
        args += [k, vt]
    return pl.pallas_call(
        functools.partial(_mla_kernel, n_kb=n_kb, tk=tk),
        grid=(batch, n_pairs, n_q),
        in_specs=in_specs,
        out_specs=pl.BlockSpec((tq, LANES), lambda b, p, i: (b * n_q + i, p)),
        out_shape=jax.ShapeDtypeStruct((batch * q_len, MLA_HEADS * MLA_V), BF16),
        compiler_params=_params(),
        name="mla_attention" if n_kb else "mla_ctx_attention",
    )(*args)


def _na_ctx_kernel(q_ref, k_ref, vt_ref, o_ref):
    cols = lambda hp: slice(hp * LANES, (hp + 1) * LANES)

    def scores(hp):
        return _pair_scores(q_ref[:, cols(hp)], [(k_ref[:, cols(hp)], None)])

    def finish(hp, sts):
        o_ref[:, cols(hp)] = _pair_softmax_pv(sts, [vt_ref[cols(hp), :]]).astype(BF16)

    _pairs_pipelined(NA_HEADS // 2, scores, finish)


def _na_ctx_attention(q, k, vt, batch, ctx_len):
    spec = pl.BlockSpec((ctx_len, NA_WIDTH), lambda b: (b, 0))
    return pl.pallas_call(
        _na_ctx_kernel,
        grid=(batch,),
        in_specs=[spec, spec, pl.BlockSpec((NA_WIDTH, ctx_len), lambda b: (0, b))],
        out_specs=spec,
        out_shape=jax.ShapeDtypeStruct((batch * ctx_len, NA_WIDTH), BF16),
        compiler_params=_params(),
        name="na_ctx_attention",
    )(q, k, vt)


FFN_HIDDEN_CHUNK = 512


def _merge_ffn_kernel(x_ref, yna_ref, ymla_ref, gmu_ref, gmv_ref, gates_ref, gt1_ref, sh2_ref, sc2_ref, gt2_ref,
                      gn2_ref, wna_ref, wgm_ref, wmla_ref, wout_ref, w1_ref, w2_ref, ws_ref, bsb_ref, o_ref,
                      ygm_ref):
    tm, d = x_ref.shape
    for c in range(tm // GM_CHUNK):
        rs = slice(c * GM_CHUNK, (c + 1) * GM_CHUNK)
        for g in range(GM_GROUPS):
            cs = slice(g * GM_GROUP_DIM, (g + 1) * GM_GROUP_DIM)
            mixed = _dot(ws_ref[g], gmv_ref[rs, cs]) + bsb_ref[:, cs]
            ygm_ref[rs, cs] = (gmu_ref[rs, cs].astype(F32) * mixed).astype(BF16)

    y = gates_ref[:, 0:d].astype(F32) * _dot(yna_ref[...], wna_ref[...])
    y = y + gates_ref[:, d:2 * d].astype(F32) * _dot(ygm_ref[...], wgm_ref[...])
    y = y + gates_ref[:, 2 * d:3 * d].astype(F32) * _dot(ymla_ref[...], wmla_ref[...])
    xn = x_ref[...] + gt1_ref[...] * _dot(y.astype(BF16), wout_ref[...])

    h2 = _rms(xn, d) * gn2_ref[...]
    h2 = (h2 * (1.0 + sc2_ref[...]) + sh2_ref[...]).astype(BF16)
    hidden = w1_ref.shape[1]
    acc = None
    for c in range(hidden // FFN_HIDDEN_CHUNK):
        cs = slice(c * FFN_HIDDEN_CHUNK, (c + 1) * FFN_HIDDEN_CHUNK)
        a = jnp.maximum(_dot(h2, w1_ref[:, cs]), 0.0)
        part = _dot((a * a).astype(BF16), w2_ref[cs, :])
        acc = part if acc is None else acc + part
    o_ref[...] = xn + gt2_ref[...] * acc


def _merge_ffn(xs, y_na, y_mla, gm_u, gm_v, gates, mod_l, mod_row_fn, tm, gn2, lw):
    r, d = xs.shape
    hidden = lw["w1"].shape[-1]
    wl = lambda shape: _resident_layer(shape, lw["layer"])
    row = lambda w: pl.BlockSpec((tm, w), lambda i: (i, 0))
    modspec = lambda chunk: pl.BlockSpec((None, None, 1, d), lambda i: (mod_row_fn(i), chunk, 0, 0))
    in_specs = [
        row(d), row(NA_WIDTH), row(MLA_HEADS * MLA_V), row(GM_WIDTH), row(GM_WIDTH), row(N_BRANCHES * d),
        modspec(2), modspec(3), modspec(4), modspec(5), _resident((1, d)),
        wl((NA_WIDTH, d)), wl((GM_WIDTH, d)), wl((MLA_HEADS * MLA_V, d)), wl((d, d)),
        wl((d, hidden)), wl((hidden, d)), _resident((GM_GROUPS, GM_CHUNK, GM_CHUNK)),
        _resident((GM_CHUNK, GM_WIDTH)),
    ]
    return pl.pallas_call(
        _merge_ffn_kernel,
        grid=(r // tm,),
        in_specs=in_specs,
        out_specs=row(d),
        out_shape=jax.ShapeDtypeStruct((r, d), F32),
        scratch_shapes=[pltpu.VMEM((tm, GM_WIDTH), BF16)],
        compiler_params=_params(),
        name="merge_ffn",
    )(xs, y_na, y_mla, gm_u, gm_v, gates, mod_l, mod_l, mod_l, mod_l, gn2, lw["w_na_o"], lw["w_gm_o"],
      lw["w_mla_o"], lw["w_out"], lw["w1"], lw["w2"], lw["w_s"], lw["b_s"])


def _rope_partner():
    perm = np.arange(MLA_QK)
    for u in range(MLA_ROPE):
        seg, half, pair = u // (2 * ROPE_AXIS_PAIRS), (u // ROPE_AXIS_PAIRS) % 2, u % ROPE_AXIS_PAIRS
        perm[MLA_NOPE + u] = MLA_NOPE + seg * 2 * ROPE_AXIS_PAIRS + (1 - half) * ROPE_AXIS_PAIRS + pair
    tail = np.arange(MLA_QK) >= MLA_NOPE
    return perm, tail


def _pad_heads(t):
    pad = [(0, 0)] * (t.ndim - 1) + [(0, MLA_SLOT - MLA_QK)]
    t = jnp.pad(t, pad)
    return t.reshape(t.shape[:-2] + (MLA_HEADS * MLA_SLOT,))


def _rope_tables(seq):
    t = jnp.arange(seq)
    rows = (t // GRID_W).astype(F32)
    colsv = (t % GRID_W).astype(F32)
    freqs = ROPE_THETA ** (-jnp.arange(ROPE_AXIS_PAIRS, dtype=F32) / ROPE_AXIS_PAIRS)
    ang = jnp.stack([rows[:, None] * freqs, colsv[:, None] * freqs], axis=1)
    cos, sin = jnp.cos(ang), jnp.sin(ang)
    cos_t = jnp.stack([cos, cos], axis=2).reshape(seq, MLA_ROPE)
    sin_t = jnp.stack([-sin, sin], axis=2).reshape(seq, MLA_ROPE)
    ones = jnp.ones((seq, MLA_NOPE), F32)
    padc = jnp.ones((seq, MLA_SLOT - MLA_QK), F32)
    cos_full = jnp.concatenate([ones, cos_t, padc], axis=1)
    sin_full = jnp.concatenate([0 * ones, sin_t, 0 * padc], axis=1)
    return cos_full, sin_full


def _pack_stacked(p, off):
    w = p["w_in"].astype(BF16)
    depth, d = w.shape[0], w.shape[1]
    cols = lambda name, width: w[:, :, off[name]:off[name] + width]
    kr = cols("kr", MLA_ROPE)
    w_in = jnp.concatenate([
        cols("naq", NA_WIDTH), cols("nak", NA_WIDTH), cols("ckv", MLA_KV_RANK),
        kr, kr, jnp.zeros((depth, d, LANES - 2 * MLA_ROPE), BF16),
        cols("cq", MLA_Q_RANK), cols("gmu", GM_WIDTH), cols("gmv", GM_WIDTH), cols("gates", N_BRANCHES * d),
    ], axis=2)
    w_nav = p["w_in"][:, :, off["nav"]:off["nav"] + NA_WIDTH].astype(BF16)
    return {
        "w_in": w_in, "w_nav": w_nav,
        "w_na_o": p["na_w_o"].astype(BF16), "w_gm_o": p["gm_w_o"].astype(BF16),
        "w_mla_o": p["mla_w_o"].astype(BF16), "w_out": p["w_out"].astype(BF16),
        "w1": p["ffn_w1"].astype(BF16), "w2": p["ffn_w2"].astype(BF16),
    }


def _pack_layer(i, p, stacked):
    perm, tail = _rope_partner()
    q_scale = MLA_QK ** -0.5 * LOG2E
    w_uq = p["mla_w_uq"][i].reshape(MLA_Q_RANK, MLA_HEADS, MLA_QK)
    wq = _pad_heads(w_uq).astype(BF16)
    wqp = _pad_heads(w_uq[:, :, perm] * tail).astype(BF16)
    qg = p["mla_q_gain"][i]
    gq = jnp.pad(qg, (0, MLA_SLOT - MLA_QK))[None] * q_scale
    gqp = jnp.pad(qg[perm] * tail, (0, MLA_SLOT - MLA_QK))[None] * q_scale

    w_ukv = p["mla_w_ukv"][i].reshape(MLA_KV_RANK, MLA_HEADS, MLA_NOPE + MLA_V)
    k_nope = jnp.pad(w_ukv[:, :, :MLA_NOPE], ((0, 0), (0, 0), (0, MLA_SLOT - MLA_NOPE)))
    k_nope = k_nope.reshape(MLA_KV_RANK, MLA_HEADS * MLA_SLOT)
    sel = np.zeros((LANES, MLA_HEADS, MLA_SLOT), np.float32)
    selp = np.zeros((LANES, MLA_HEADS, MLA_SLOT), np.float32)
    for t in range(MLA_ROPE):
        for copy in range(2):
            sel[copy * MLA_ROPE + t, :, MLA_NOPE + t] = 1.0
            selp[copy * MLA_ROPE + t, :, perm[MLA_NOPE + t]] = 1.0
    wk = jnp.concatenate([k_nope, sel.reshape(LANES, -1)], axis=0).astype(BF16)
    wkp = jnp.concatenate([jnp.zeros_like(k_nope), selp.reshape(LANES, -1)], axis=0).astype(BF16)
    kg = p["mla_k_gain"][i]
    gk = jnp.pad(kg, (0, MLA_SLOT - MLA_QK))[None]
    gkp = jnp.pad(kg[perm] * tail, (0, MLA_SLOT - MLA_QK))[None]
    wvt = w_ukv[:, :, MLA_NOPE:].reshape(MLA_KV_RANK, MLA_HEADS * MLA_V).T.astype(BF16)

    blk = np.arange(2 * LANES) // NA_HEAD_DIM
    bd = jnp.asarray(blk[:, None] == blk[None, :], BF16)

    return {
        **stacked, "layer": i,
        "w_navt": stacked["w_nav"][i].T,
        "na_qg": jnp.tile(p["na_q_gain"][i], NA_HEADS)[None] * (NA_HEAD_DIM ** -0.5 * LOG2E),
        "na_kg": jnp.tile(p["na_k_gain"][i], NA_HEADS)[None],
        "bd": bd,
        "ckv_g": p["mla_ckv_gain"][i][None], "cq_g": p["mla_cq_gain"][i][None],
        "ln_g": p["gm_ln_g"][i][None], "ln_b": p["gm_ln_b"][i][None],
        "wq": wq, "wqp": wqp, "gq": gq, "gqp": gqp, "wk": wk, "wkp": wkp, "gk": gk, "gkp": gkp, "wvt": wvt,
        "w_s": p["gm_w_s"][i].astype(BF16),
        "b_s": jnp.repeat(p["gm_b_s"][i].T, GM_GROUP_DIM, axis=1),
    }


def _in_offsets(d):
    off = {"nak": 0}
    off["nav"] = off["nak"] + NA_WIDTH
    off["ckv"] = off["nav"] + NA_WIDTH
    off["kr"] = off["ckv"] + MLA_KV_RANK
    off["naq"] = off["kr"] + MLA_ROPE
    off["gmu"] = off["naq"] + NA_WIDTH
    off["gmv"] = off["gmu"] + GM_WIDTH
    off["cq"] = off["gmv"] + GM_WIDTH
    off["gates"] = off["cq"] + MLA_Q_RANK
    return off


def kernel(x, c, ctx, c_ctx, w_mod, b_mod, g_norm1, g_norm2, w_in, na_q_gain, na_k_gain, na_rpb, na_w_o, gm_ln_g,
           gm_ln_b, gm_w_s, gm_b_s, gm_w_o, mla_cq_gain, mla_ckv_gain, mla_w_uq, mla_w_ukv, mla_q_gain, mla_k_gain,
           mla_w_o, w_out, ffn_w1, ffn_w2):
    batch, seq, d = x.shape
    ctx_len = ctx.shape[1]
    depth = w_mod.shape[0]
    p = dict(w_in=w_in, na_q_gain=na_q_gain, na_k_gain=na_k_gain, na_w_o=na_w_o, gm_ln_g=gm_ln_g, gm_ln_b=gm_ln_b,
             gm_w_s=gm_w_s, gm_b_s=gm_b_s, gm_w_o=gm_w_o, mla_cq_gain=mla_cq_gain, mla_ckv_gain=mla_ckv_gain,
             mla_w_uq=mla_w_uq, mla_w_ukv=mla_w_ukv, mla_q_gain=mla_q_gain, mla_k_gain=mla_k_gain, mla_w_o=mla_w_o,
             w_out=w_out, ffn_w1=ffn_w1, ffn_w2=ffn_w2)
    stacked = _pack_stacked(p, _in_offsets(d))

    mod_rows = 16
    ctx_row = batch
    rows = jnp.concatenate([c, c_ctx[None], jnp.zeros((mod_rows - batch - 1, d), F32)], axis=0)
    mod = _modulation(rows, w_mod, b_mod).reshape(depth, mod_rows, 6, 1, d)

    tm = min(ROW_TILE, seq)
    tmc = min(CTX_ROW_TILE, batch * ctx_len)
    cos_t, sin_t = _rope_tables(seq)
    cos_c = jnp.ones((tmc, LANES), F32)
    sin_c = jnp.zeros((tmc, LANES), F32)
    tiles_per_seq = seq // tm
    tiles_per_ctx = 1
    lat_row = lambda i: i // tiles_per_seq
    ctx_row_fn = lambda i: ctx_row
    grid_rows = seq // GRID_W
    na_bias = _na_bias_tables(na_rpb, grid_rows)

    xs = x.reshape(batch * seq, d)
    cs = ctx.reshape(batch * ctx_len, d)
    for i in range(depth):
        last = i == depth - 1
        lw = _pack_layer(i, p, stacked)
        mod_l = mod[i]
        gn1 = g_norm1[i][None]
        gn2 = g_norm2[i][None]

        (naq, nak, navt, gmu, gmv, gates, qm, km, vmt) = _in_projection(
            xs, mod_l, lat_row, tm, tiles_per_seq, gn1, lw, cos_t, sin_t)
        (naq_c, nak_c, navt_c, gmu_c, gmv_c, gates_c, qm_c, km_c, vmt_c) = _in_projection(
            cs, mod_l, ctx_row_fn, tmc, tiles_per_ctx, gn1, lw, cos_c, sin_c)

        y_na = _na_attention(naq, nak, navt, nak_c, navt_c, na_bias, i, batch, seq, ctx_len)
        y_mla = _mla_attention(qm, km_c, vmt_c, km, vmt, batch, seq, ctx_len, seq, tq=min(MLA_TQ, seq),
                               tk=min(MLA_TK, seq))
        xs_new = _merge_ffn(xs, y_na, y_mla, gmu, gmv, gates, mod_l, lat_row, tm, gn2, lw)

        if not last:
            y_na_c = _na_ctx_attention(naq_c, nak_c, navt_c, batch, ctx_len)
            y_mla_c = _mla_attention(qm_c, km_c, vmt_c, None, None, batch, ctx_len, ctx_len, 0, tq=ctx_len, tk=0)
            cs = _merge_ffn(cs, y_na_c, y_mla_c, gmu_c, gmv_c, gates_c, mod_l, ctx_row_fn, tmc, gn2, lw)
        xs = xs_new
    return xs.reshape(batch, seq, d)
```

```python
import functools
import math

import numpy as np
import jax
import jax.numpy as jnp
from jax import lax
from jax.experimental import pallas as pl
from jax.experimental.pallas import tpu as pltpu

F32 = jnp.float32
BF16 = jnp.bfloat16

LANES = 128
BF16_SUBLANES = 16
VMEM_LIMIT_BYTES = 60 * 1024 * 1024

GRID_W = 64
N_BRANCHES = 3
NA_HEADS = 8
NA_HEAD_DIM = 64
NA_WIDTH = NA_HEADS * NA_HEAD_DIM
NA_WIN_H = 8
NA_WIN_W = 16
GM_GROUPS = 4
GM_CHUNK = 128
GM_GROUP_DIM = 128
GM_WIDTH = GM_GROUPS * GM_GROUP_DIM
MLA_HEADS = 8
MLA_Q_RANK = 256
MLA_KV_RANK = 128
MLA_NOPE = 64
MLA_ROPE = 32
MLA_V = 64
MLA_QK = MLA_NOPE + MLA_ROPE
ROPE_AXIS_PAIRS = MLA_ROPE // 4
ROPE_THETA = 10000.0
NORM_EPS = 1e-6
NEG_INF = -1e30
LOG2E = math.log2(math.e)

NA_QR = 4
NA_BAND = 12
NA_BLOCKS_PER_STEP = 4
MLA_SLOT = LANES
ROW_TILE = 512
CTX_ROW_TILE = 512
MLA_TQ = 2048
MLA_TK = 256
MLA_CHAIN_Q = 256
GATE_CHUNK = 512
MOD_COL_TILE = 1536
MOD_ROWS = 16


def _resident(shape):
    zeros = (0,) * len(shape)
    return pl.BlockSpec(shape, lambda *_: zeros, pipeline_mode=pl.Buffered(1))


def _resident_layer(shape, layer):
    zeros = (0,) * len(shape)
    return pl.BlockSpec((None,) + tuple(shape), lambda *_: (layer,) + zeros, pipeline_mode=pl.Buffered(1))


def _params():
    return pltpu.CompilerParams(vmem_limit_bytes=VMEM_LIMIT_BYTES)


def _dot(a, b):
    return jnp.dot(a, b, preferred_element_type=F32)


def _dot_t(a, b):
    return lax.dot_general(a, b, (((1,), (1,)), ((), ())), preferred_element_type=F32)


def _mod_kernel(s_ref, w_ref, b_ref, o_ref):
    s = s_ref[...]
    s = s * jax.nn.sigmoid(s)
    o_ref[...] = _dot(s.astype(BF16), w_ref[...].astype(BF16)) + b_ref[...]


def _modulation(rows, w_mod, b_mod):
    depth, d, n = w_mod.shape
    r = rows.shape[0]
    tn = MOD_COL_TILE
    return pl.pallas_call(
        _mod_kernel,
        grid=(depth, n // tn),
        in_specs=[
            pl.BlockSpec((r, d), lambda l, j: (0, 0)),
            pl.BlockSpec((None, d, tn), lambda l, j: (l, 0, j)),
            pl.BlockSpec((None, 1, tn), lambda l, j: (l, 0, j)),
        ],
        out_specs=pl.BlockSpec((None, r, tn), lambda l, j: (l, 0, j)),
        out_shape=jax.ShapeDtypeStruct((depth, r, n), F32),
        compiler_params=_params(),
        name="modulation",
    )(rows, w_mod, b_mod.reshape(depth, 1, n))


C_NAQ = 0
C_NAK = C_NAQ + NA_WIDTH
C_CKV = C_NAK + NA_WIDTH
C_KR = C_CKV + MLA_KV_RANK
C_CQ = C_KR + LANES
C_GMU = C_CQ + MLA_Q_RANK
C_GMV = C_GMU + GM_WIDTH
C_GATES = C_GMV + GM_WIDTH
C_TOTAL = C_GATES + N_BRANCHES * 1024


def _rms(t, width):
    return t * lax.rsqrt(jnp.sum(t * t, axis=-1, keepdims=True) * (1.0 / width) + NORM_EPS)


def _gelu(t):
    return 0.5 * t * (1.0 + lax.erf(t * math.sqrt(0.5)))


def _sigmoid(t):
    return 0.5 * jnp.tanh(0.5 * t) + 0.5


def _head64_norm(p, bd):
    p2 = (p * p).astype(BF16)
    half = 2 * LANES
    ss = jnp.concatenate([_dot(p2[:, :half], bd), _dot(p2[:, half:], bd)], axis=1)
    return p * lax.rsqrt(ss * (1.0 / NA_HEAD_DIM) + NORM_EPS)


def _inproj_kernel(x_ref, sh_ref, sc_ref, gn_ref, w_ref, wnavt_ref, qg_ref, kg_ref, bd_ref, ckvg_ref, cqg_ref, lng_ref,
                   lnb_ref, wq_ref, wqp_ref, gq_ref, gqp_ref, wk_ref, wkp_ref, gk_ref, gkp_ref, wvt_ref, cos_ref,
                   sin_ref, naq_ref, nak_ref, navt_ref, gmu_ref, gmv_ref, gates_ref, qm_ref, km_ref, vmt_ref):
    x = x_ref[...]
    d = x.shape[-1]
    h = _rms(x, d) * gn_ref[...]
    h = h * (1.0 + sc_ref[...]) + sh_ref[...]
    hb = h.astype(BF16)

    def proj(c0, width):
        return _dot(hb, w_ref[:, c0:c0 + width])

    p_ckv = proj(C_CKV, MLA_KV_RANK)
    p_cq = proj(C_CQ, MLA_Q_RANK)
    kr = proj(C_KR, LANES)
    p_naq = proj(C_NAQ, NA_WIDTH)
    p_nak = proj(C_NAK, NA_WIDTH)
    navt_ref[...] = _dot_t(wnavt_ref[...], hb).astype(BF16)

    ckv_n = (_rms(p_ckv, MLA_KV_RANK) * ckvg_ref[...]).astype(BF16)
    cq_n = (_rms(p_cq, MLA_Q_RANK) * cqg_ref[...]).astype(BF16)
    kr_hi = kr.astype(BF16)
    kr_lo = (kr - kr_hi.astype(F32)).astype(BF16)
    lane = lax.broadcasted_iota(jnp.int32, (1, LANES), 1)
    kr_hl = jnp.where(lane < MLA_ROPE, kr_hi, kr_lo)
    lhs_k = jnp.concatenate([ckv_n, kr_hl], axis=1)

    p_gmu = proj(C_GMU, GM_WIDTH)
    p_gmv = proj(C_GMV, GM_WIDTH)

    bd = bd_ref[...]
    naq_ref[...] = (_head64_norm(p_naq, bd) * qg_ref[...]).astype(BF16)
    nak_ref[...] = (_head64_norm(p_nak, bd) * kg_ref[...]).astype(BF16)

    vmt_ref[...] = _dot_t(wvt_ref[...], ckv_n).astype(BF16)
    xq, xqp = _dot(cq_n, wq_ref[...]), _dot(cq_n, wqp_ref[...])
    xk, xkp = _dot(lhs_k, wk_ref[...]), _dot(lhs_k, wkp_ref[...])

    gmu_ref[...] = _gelu(p_gmu).astype(BF16)
    gv = _gelu(p_gmv)
    mu = jnp.mean(gv, axis=-1, keepdims=True)
    gc = gv - mu
    var = jnp.mean(gc * gc, axis=-1, keepdims=True)
    gmv_ref[...] = (gc * lax.rsqrt(var + NORM_EPS) * lng_ref[...] + lnb_ref[...]).astype(BF16)

    n_gate = gates_ref.shape[-1]
    gate_chunk = GATE_CHUNK
    p_gate = proj(C_GATES, gate_chunk)

    cosv = cos_ref[...]
    sinv = sin_ref[...]

    def heads(xa, xpa, g_r, gp_r, out_r):
        gcos, gsin = g_r[...] * cosv, gp_r[...] * sinv
        for hd in range(MLA_HEADS):
            sl = slice(hd * MLA_SLOT, (hd + 1) * MLA_SLOT)
            xh = xa[:, sl]
            r = lax.rsqrt(jnp.sum(xh * xh, axis=-1, keepdims=True) * (1.0 / MLA_QK) + NORM_EPS)
            out_r[:, sl] = (r * (xh * gcos + xpa[:, sl] * gsin)).astype(BF16)

    heads(xq, xqp, gq_ref, gqp_ref, qm_ref)
    heads(xk, xkp, gk_ref, gkp_ref, km_ref)

    for j in range(n_gate // gate_chunk):
        nxt = proj(C_GATES + (j + 1) * gate_chunk, gate_chunk) if (j + 1) * gate_chunk < n_gate else None
        gates_ref[:, j * gate_chunk:(j + 1) * gate_chunk] = _sigmoid(p_gate).astype(BF16)
        p_gate = nxt


def _in_projection(xs, mod_l, mod_row_fn, tm, tiles_per_seq, gn, lw, cos_t, sin_t):
    r, d = xs.shape
    n_tiles = r // tm
    row = lambda w: pl.BlockSpec((tm, w), lambda i: (i, 0))
    modspec = lambda chunk: pl.BlockSpec((None, None, 1, d), lambda i: (mod_row_fn(i), chunk, 0, 0))
    pos = pl.BlockSpec((tm, LANES), lambda i: (i % tiles_per_seq, 0))
    in_specs = [
        row(d), modspec(0), modspec(1), _resident((1, d)), _resident_layer((d, C_TOTAL), lw["layer"]),
        _resident((NA_WIDTH, d)),
        _resident((1, NA_WIDTH)), _resident((1, NA_WIDTH)), _resident((2 * LANES, 2 * LANES)),
        _resident((1, MLA_KV_RANK)), _resident((1, MLA_Q_RANK)), _resident((1, GM_WIDTH)), _resident((1, GM_WIDTH)),
        _resident((MLA_Q_RANK, MLA_HEADS * MLA_SLOT)), _resident((MLA_Q_RANK, MLA_HEADS * MLA_SLOT)),
        _resident((1, MLA_SLOT)), _resident((1, MLA_SLOT)),
        _resident((2 * LANES, MLA_HEADS * MLA_SLOT)), _resident((2 * LANES, MLA_HEADS * MLA_SLOT)),
        _resident((1, MLA_SLOT)), _resident((1, MLA_SLOT)),
        _resident((MLA_HEADS * MLA_V, MLA_KV_RANK)), pos, pos,
    ]
    outs = [("naq", NA_WIDTH, False), ("nak", NA_WIDTH, False), ("navt", NA_WIDTH, True), ("gmu", GM_WIDTH, False),
            ("gmv", GM_WIDTH, False), ("gates", N_BRANCHES * d, False), ("qm", MLA_HEADS * MLA_SLOT, False),
            ("km", MLA_HEADS * MLA_SLOT, False), ("vmt", MLA_HEADS * MLA_V, True)]
    col = lambda w: pl.BlockSpec((w, tm), lambda i: (0, i))
    return pl.pallas_call(
        _inproj_kernel,
        grid=(n_tiles,),
        in_specs=in_specs,
        out_specs=[col(w) if t else row(w) for _, w, t in outs],
        out_shape=[jax.ShapeDtypeStruct((w, r) if t else (r, w), BF16) for _, w, t in outs],
        compiler_params=_params(),
        name="in_projection",
    )(xs, mod_l, mod_l, gn, lw["w_in"], lw["w_navt"], lw["na_qg"], lw["na_kg"], lw["bd"], lw["ckv_g"], lw["cq_g"],
      lw["ln_g"],
      lw["ln_b"], lw["wq"], lw["wqp"], lw["gq"], lw["gqp"], lw["wk"], lw["wkp"], lw["gk"], lw["gkp"], lw["wvt"],
      cos_t, sin_t)


def _pair_scores(q_pair, k_blocks):
    lane = lax.broadcasted_iota(jnp.int32, (1, LANES), 1)
    zero = jnp.zeros_like(q_pair)
    q2 = jnp.concatenate([jnp.where(lane < NA_HEAD_DIM, q_pair, zero), jnp.where(lane >= NA_HEAD_DIM, q_pair, zero)],
                         axis=0)
    return [_dot_t(k_pair, q2) if bias_t is None else _dot_t(k_pair, q2) + bias_t for k_pair, bias_t in k_blocks]


def _pair_softmax_pv(sts, vt_blocks):
    nq = sts[0].shape[1] // 2
    m = None
    for st in sts:
        mb = jnp.max(st, axis=0, keepdims=True)
        m = mb if m is None else jnp.maximum(m, mb)
    acc = None
    for st, vt_pair in zip(sts, vt_blocks):
        ones = jnp.ones((BF16_SUBLANES, vt_pair.shape[1]), BF16)
        part = _dot(jnp.concatenate([vt_pair, ones], axis=0), jnp.exp2(st - m).astype(BF16))
        acc = part if acc is None else acc + part
    o0 = acc[:NA_HEAD_DIM, :nq] / acc[LANES:LANES + 1, :nq]
    o1 = acc[NA_HEAD_DIM:LANES, nq:] / acc[LANES:LANES + 1, nq:]
    return jnp.concatenate([o0, o1], axis=0).T


def _pairs_pipelined(n_pairs, scores_fn, finish_fn):
    all_scores = [scores_fn(p) for p in range(n_pairs)]
    for p in range(n_pairs):
        finish_fn(p, all_scores[p])


def _na_kernel(q_ref, k_ref, vt_ref, kc_ref, vct_ref, bias_ref, o_ref, *, grid_rows, blocks_per_step):
    n_rb = grid_rows // NA_QR
    nk = NA_BAND * GRID_W
    tq = NA_QR * GRID_W
    n_pairs = NA_HEADS // 2
    cols = lambda hp: slice(hp * LANES, (hp + 1) * LANES)

    def geometry(sub):
        rb = pl.program_id(1) * blocks_per_step + sub
        cls = jnp.where(rb == 0, 0, jnp.where(rb == n_rb - 1, 2, 1))
        band0 = jnp.clip(NA_QR * rb - NA_WIN_H // 2, 0, grid_rows - NA_BAND)
        return cls, pl.multiple_of(band0 * GRID_W, 2 * LANES)

    geo = [geometry(sub) for sub in range(blocks_per_step)]

    def scores(c):
        sub, hp = divmod(c, n_pairs)
        cls, k0 = geo[sub]
        cs = cols(hp)
        k_all = jnp.concatenate([k_ref[pl.ds(k0, nk), cs], kc_ref[:, cs]], axis=0)
        st = _pair_scores(q_ref[sub * tq:(sub + 1) * tq, cs], [(k_all, None)])[0]
        return [jnp.concatenate([st[:nk] + bias_ref[cls, hp], st[nk:]], axis=0)]

    def finish(c, sts):
        sub, hp = divmod(c, n_pairs)
        _, k0 = geo[sub]
        cs = cols(hp)
        vt_all = jnp.concatenate([vt_ref[cs, pl.ds(k0, nk)], vct_ref[cs, :]], axis=1)
        o_ref[sub * tq:(sub + 1) * tq, cs] = _pair_softmax_pv(sts, [vt_all]).astype(BF16)

    for sub in range(blocks_per_step):
        _pairs_pipelined(n_pairs, lambda p, sub=sub: scores(sub * n_pairs + p),
                         lambda p, sts, sub=sub: finish(sub * n_pairs + p, sts))


def _na_attention(q, k, vt, kc, vct, bias, layer, batch, seq, ctx_len):
    grid_rows = seq // GRID_W
    blocks_per_step = math.gcd(NA_BLOCKS_PER_STEP, grid_rows // NA_QR)
    n_rb = grid_rows // NA_QR // blocks_per_step
    tq = NA_QR * GRID_W * blocks_per_step
    nk = NA_BAND * GRID_W

    return pl.pallas_call(
        functools.partial(_na_kernel, grid_rows=grid_rows, blocks_per_step=blocks_per_step),
        grid=(batch, n_rb),
        in_specs=[
            pl.BlockSpec((tq, NA_WIDTH), lambda b, rb: (b * n_rb + rb, 0)),
            pl.BlockSpec((seq, NA_WIDTH), lambda b, rb: (b, 0)),
            pl.BlockSpec((NA_WIDTH, seq), lambda b, rb: (0, b)),
            pl.BlockSpec((ctx_len, NA_WIDTH), lambda b, rb: (b, 0)),
            pl.BlockSpec((NA_WIDTH, ctx_len), lambda b, rb: (0, b)),
            pl.BlockSpec((None, 3, NA_HEADS // 2, nk, 2 * NA_QR * GRID_W), lambda b, rb: (layer, 0, 0, 0, 0),
                         pipeline_mode=pl.Buffered(1)),
        ],
        out_specs=pl.BlockSpec((tq, NA_WIDTH), lambda b, rb: (b * n_rb + rb, 0)),
        out_shape=jax.ShapeDtypeStruct((batch * seq, NA_WIDTH), BF16),
        compiler_params=_params(),
        name="na_attention",
    )(q, k, vt, kc, vct, bias)


NA_DR = 2 * NA_WIN_H - 1
NA_DC = 2 * NA_WIN_W - 1


def _na_window_geometry(grid_rows):
    n_rb = grid_rows // NA_QR
    kh, kw = NA_WIN_H, NA_WIN_W
    cq = np.arange(GRID_W)
    c0 = np.clip(cq - kw // 2, 0, GRID_W - kw)
    col_in = (cq[None, :] >= c0[:, None]) & (cq[None, :] < c0[:, None] + kw)
    dc = np.clip(cq[None, :] - cq[:, None], -(kw - 1), kw - 1) + kw - 1
    dc_t = np.tile(dc.T, (1, 2)).astype(np.int32)
    col_in_t = np.tile(col_in.T, (1, 2)).astype(np.int32)
    dr = np.zeros((3, NA_QR, NA_BAND), np.int32)
    ok = np.zeros((3, NA_QR, NA_BAND), bool)
    for c, rb in enumerate((0, 1, n_rb - 1)):
        band0 = int(np.clip(NA_QR * rb - kh // 2, 0, grid_rows - NA_BAND))
        for i in range(NA_QR):
            rq = NA_QR * rb + i
            r0 = int(np.clip(rq - kh // 2, 0, grid_rows - kh))
            for j in range(NA_BAND):
                kr = band0 + j
                ok[c, i, j] = r0 <= kr < r0 + kh
                dr[c, i, j] = int(np.clip(kr - rq + kh - 1, 0, NA_DR - 1))
    return dc_t, col_in_t, dr, ok


def _na_bias_kernel(rpb_ref, dc_ref, colin_ref, o_ref, tile_ref, *, dr, ok):
    layer = pl.program_id(0)
    dc = dc_ref[...]
    col_in = colin_ref[...] > 0
    n_tiles = NA_HEADS * NA_DR

    def build(t, carry):
        base = (layer * n_tiles + t) * NA_DC
        acc = jnp.zeros(dc.shape, F32)
        for d in range(NA_DC):
            acc = jnp.where(dc == d, rpb_ref[base + d], acc)
        tile_ref[t] = jnp.where(col_in, acc * LOG2E, NEG_INF)
        return carry

    lax.fori_loop(0, n_tiles, build, 0)

    lane = lax.broadcasted_iota(jnp.int32, (1, LANES), 1)
    neg = jnp.full((GRID_W, LANES), NEG_INF, F32)
    for c in range(3):
        for hp in range(NA_HEADS // 2):
            for j in range(NA_BAND):
                for g in range(2 * NA_QR // 2):
                    head = 2 * hp + g // (NA_QR // 2)
                    i0 = 2 * (g % (NA_QR // 2))
                    halves = [tile_ref[head * NA_DR + int(dr[c, i, j])] if ok[c, i, j] else neg for i in (i0, i0 + 1)]
                    o_ref[c, hp, j * GRID_W:(j + 1) * GRID_W, g * LANES:(g + 1) * LANES] = jnp.where(
                        lane < GRID_W, halves[0], halves[1])


def _na_bias_tables(na_rpb, grid_rows):
    depth = na_rpb.shape[0]
    dc_t, col_in_t, dr, ok = _na_window_geometry(grid_rows)
    nk, nq2 = NA_BAND * GRID_W, 2 * NA_QR * GRID_W
    return pl.pallas_call(
        functools.partial(_na_bias_kernel, dr=dr, ok=ok),
        grid=(depth,),
        in_specs=[
            pl.BlockSpec(memory_space=pltpu.SMEM),
            pl.BlockSpec((GRID_W, LANES), lambda l: (0, 0)),
            pl.BlockSpec((GRID_W, LANES), lambda l: (0, 0)),
        ],
        out_specs=pl.BlockSpec((None, 3, NA_HEADS // 2, nk, nq2), lambda l: (l, 0, 0, 0, 0)),
        out_shape=jax.ShapeDtypeStruct((depth, 3, NA_HEADS // 2, nk, nq2), F32),
        scratch_shapes=[pltpu.VMEM((NA_HEADS * NA_DR, GRID_W, LANES), F32)],
        compiler_params=_params(),
        name="na_bias_tables",
    )(na_rpb.reshape(-1), jnp.asarray(dc_t), jnp.asarray(col_in_t))


def _mla_kernel(q_ref, kc_ref, vct_ref, *rest, n_kb, tk):
    if n_kb:
        k_ref, vt_ref, o_ref = rest
    else:
        (o_ref,) = rest
    tq = q_ref.shape[0]
    pairs = q_ref.shape[1] // (2 * MLA_SLOT)
    slot = lambda pp, h: slice((2 * pp + h) * MLA_SLOT, (2 * pp + h + 1) * MLA_SLOT)
    pair_rows = lambda pp: slice(pp * 2 * MLA_V, (pp + 1) * 2 * MLA_V)
    lo = lax.broadcasted_iota(jnp.int32, (2 * MLA_V, 1), 0) < MLA_V

    def values(vt_blk):
        one = jnp.ones_like(vt_blk)
        return jnp.where(lo, vt_blk, one).astype(F32), jnp.where(lo, one, vt_blk).astype(F32)

    def update(st, m, acc, vt_aug):
        m_new = jnp.maximum(m, jnp.max(st, axis=0, keepdims=True))
        return m_new, jnp.exp2(m - m_new) * acc + _dot(vt_aug, jnp.exp2(st - m_new))

    k_blk = lambda j, cs: kc_ref[:, cs] if j < 0 else k_ref[pl.ds(j * tk, tk), cs]
    vt_blk = lambda j, rs: vct_ref[rs, :] if j < 0 else vt_ref[rs, pl.ds(j * tk, tk)]
    blocks = list(range(-1, n_kb))
    nq = min(MLA_CHAIN_Q, tq)
    chains = [(pp, h, s) for pp in range(pairs) for s in range(tq // nq) for h in range(2)]
    q = [q_ref[s * nq:(s + 1) * nq, slot(pp, h)] for pp, h, s in chains]
    scores = lambda j, c: _dot_t(k_blk(j, slot(*chains[c][:2])), q[c])
    nxt = [scores(blocks[0], c) for c in range(len(chains))]
    m = [jnp.full((1, nq), NEG_INF, F32)] * len(chains)
    acc = [jnp.zeros((2 * MLA_V, nq), F32)] * len(chains)
    for idx, j in enumerate(blocks):
        vt_aug = [values(vt_blk(j, pair_rows(pp))) for pp in range(pairs)]
        for c, (pp, h, _) in enumerate(chains):
            st = nxt[c]
            if idx + 1 < len(blocks):
                nxt[c] = scores(blocks[idx + 1], c)
            m[c], acc[c] = update(st, m[c], acc[c], vt_aug[pp][h])
    for pp in range(pairs):
        acc_h = [jnp.concatenate([acc[c] for c, ch in enumerate(chains) if ch[:2] == (pp, hh)], axis=1)
                 for hh in range(2)]
        o0 = acc_h[0][:MLA_V] / acc_h[0][MLA_V:]
        o1 = acc_h[1][MLA_V:] / acc_h[1][:MLA_V]
        o_ref[:, pp * LANES:(pp + 1) * LANES] = jnp.concatenate([o0, o1], axis=0).T.astype(BF16)


def _mla_attention(q, kc, vct, k, vt, batch, q_len, ctx_len, kv_len, tq, tk, pairs_per_step=1):
    n_q = q_len // tq
    n_pairs = MLA_HEADS // 2 // pairs_per_step
    n_kb = kv_len // tk if k is not None else 0
    qk_w, v_w = pairs_per_step * 2 * MLA_SLOT, pairs_per_step * 2 * MLA_V
    in_specs = [
        pl.BlockSpec((tq, qk_w), lambda b, p, i: (b * n_q + i, p)),
        pl.BlockSpec((ctx_len, qk_w), lambda b, p, i: (b, p)),
        pl.BlockSpec((v_w, ctx_len), lambda b, p, i: (p, b)),
    ]
    args = [q, kc, vct]
    if n_kb:
        in_specs += [
            pl.BlockSpec((kv_len, qk_w), lambda b, p, i: (b, p)),
            pl.BlockSpec((v_w, kv_len), lambda b, p, i: (p, b)),
        ]
        args += [k, vt]
    return pl.pallas_call(
        functools.partial(_mla_kernel, n_kb=n_kb, tk=tk),
        grid=(batch, n_pairs, n_q),
        in_specs=in_specs,
        out_specs=pl.BlockSpec((tq, v_w), lambda b, p, i: (b * n_q + i, p)),
        out_shape=jax.ShapeDtypeStruct((batch * q_len, MLA_HEADS * MLA_V), BF16),
        compiler_params=_params(),
        name="mla_attention" if n_kb else "mla_ctx_attention",
    )(*args)


def _na_ctx_kernel(q_ref, k_ref, vt_ref, o_ref):
    cols = lambda hp: slice(hp * LANES, (hp + 1) * LANES)

    def scores(hp):
        return _pair_scores(q_ref[:, cols(hp)], [(k_ref[:, cols(hp)], None)])

    def finish(hp, sts):
        o_ref[:, cols(hp)] = _pair_softmax_pv(sts, [vt_ref[cols(hp), :]]).astype(BF16)

    _pairs_pipelined(NA_HEADS // 2, scores, finish)


def _na_ctx_attention(q, k, vt, batch, ctx_len):
    spec = pl.BlockSpec((ctx_len, NA_WIDTH), lambda b: (b, 0))
    return pl.pallas_call(
        _na_ctx_kernel,
        grid=(batch,),
        in_specs=[spec, spec, pl.BlockSpec((NA_WIDTH, ctx_len), lambda b: (0, b))],
        out_specs=spec,
        out_shape=jax.ShapeDtypeStruct((batch * ctx_len, NA_WIDTH), BF16),
        compiler_params=_params(),
        name="na_ctx_attention",
    )(q, k, vt)


FFN_HIDDEN_CHUNK = 512


def _merge_ffn_kernel(x_ref, yna_ref, ymla_ref, gmu_ref, gmv_ref, gates_ref, gt1_ref, sh2_ref, sc2_ref, gt2_ref,
                      gn2_ref, wna_ref, wgm_ref, wmla_ref, wout_ref, w1_ref, w2_ref, ws_ref, bsb_ref, o_ref,
                      ygm_ref):
    tm, d = x_ref.shape
    for c in range(tm // GM_CHUNK):
        rs = slice(c * GM_CHUNK, (c + 1) * GM_CHUNK)
        for g in range(GM_GROUPS):
            cs = slice(g * GM_GROUP_DIM, (g + 1) * GM_GROUP_DIM)
            mixed = _dot(ws_ref[g], gmv_ref[rs, cs]) + bsb_ref[:, cs]
            ygm_ref[rs, cs] = (gmu_ref[rs, cs].astype(F32) * mixed).astype(BF16)

    y = gates_ref[:, 0:d].astype(F32) * _dot(yna_ref[...], wna_ref[...])
    y = y + gates_ref[:, d:2 * d].astype(F32) * _dot(ygm_ref[...], wgm_ref[...])
    y = y + gates_ref[:, 2 * d:3 * d].astype(F32) * _dot(ymla_ref[...], wmla_ref[...])
    xn = x_ref[...] + gt1_ref[...] * _dot(y.astype(BF16), wout_ref[...])

    h2 = _rms(xn, d) * gn2_ref[...]
    h2 = (h2 * (1.0 + sc2_ref[...]) + sh2_ref[...]).astype(BF16)
    hidden = w1_ref.shape[1]
    acc = None
    for c in range(hidden // FFN_HIDDEN_CHUNK):
        cs = slice(c * FFN_HIDDEN_CHUNK, (c + 1) * FFN_HIDDEN_CHUNK)
        a = jnp.maximum(_dot(h2, w1_ref[:, cs]), 0.0)
        part = _dot((a * a).astype(BF16), w2_ref[cs, :])
        acc = part if acc is None else acc + part
    o_ref[...] = xn + gt2_ref[...] * acc


def _merge_ffn(xs, y_na, y_mla, gm_u, gm_v, gates, mod_l, mod_row_fn, tm, gn2, lw):
    r, d = xs.shape
    hidden = lw["w1"].shape[-1]
    wl = lambda shape: _resident_layer(shape, lw["layer"])
    row = lambda w: pl.BlockSpec((tm, w), lambda i: (i, 0))
    modspec = lambda chunk: pl.BlockSpec((None, None, 1, d), lambda i: (mod_row_fn(i), chunk, 0, 0))
    in_specs = [
        row(d), row(NA_WIDTH), row(MLA_HEADS * MLA_V), row(GM_WIDTH), row(GM_WIDTH), row(N_BRANCHES * d),
        modspec(2), modspec(3), modspec(4), modspec(5), _resident((1, d)),
        wl((NA_WIDTH, d)), wl((GM_WIDTH, d)), wl((MLA_HEADS * MLA_V, d)), wl((d, d)),
        wl((d, hidden)), wl((hidden, d)), _resident((GM_GROUPS, GM_CHUNK, GM_CHUNK)),
        _resident((GM_CHUNK, GM_WIDTH)),
    ]
    return pl.pallas_call(
        _merge_ffn_kernel,
        grid=(r // tm,),
        in_specs=in_specs,
        out_specs=row(d),
        out_shape=jax.ShapeDtypeStruct((r, d), F32),
        scratch_shapes=[pltpu.VMEM((tm, GM_WIDTH), BF16)],
        compiler_params=_params(),
        name="merge_ffn",
    )(xs, y_na, y_mla, gm_u, gm_v, gates, mod_l, mod_l, mod_l, mod_l, gn2, lw["w_na_o"], lw["w_gm_o"],
      lw["w_mla_o"], lw["w_out"], lw["w1"], lw["w2"], lw["w_s"], lw["b_s"])


def _rope_partner():
    perm = np.arange(MLA_QK)
    for u in range(MLA_ROPE):
        seg, half, pair = u // (2 * ROPE_AXIS_PAIRS), (u // ROPE_AXIS_PAIRS) % 2, u % ROPE_AXIS_PAIRS
        perm[MLA_NOPE + u] = MLA_NOPE + seg * 2 * ROPE_AXIS_PAIRS + (1 - half) * ROPE_AXIS_PAIRS + pair
    tail = np.arange(MLA_QK) >= MLA_NOPE
    return perm, tail


def _pad_heads(t):
    pad = [(0, 0)] * (t.ndim - 1) + [(0, MLA_SLOT - MLA_QK)]
    t = jnp.pad(t, pad)
    return t.reshape(t.shape[:-2] + (MLA_HEADS * MLA_SLOT,))


def _rope_tables(seq):
    t = jnp.arange(seq)
    rows = (t // GRID_W).astype(F32)
    colsv = (t % GRID_W).astype(F32)
    freqs = ROPE_THETA ** (-jnp.arange(ROPE_AXIS_PAIRS, dtype=F32) / ROPE_AXIS_PAIRS)
    ang = jnp.stack([rows[:, None] * freqs, colsv[:, None] * freqs], axis=1)
    cos, sin = jnp.cos(ang), jnp.sin(ang)
    cos_t = jnp.stack([cos, cos], axis=2).reshape(seq, MLA_ROPE)
    sin_t = jnp.stack([-sin, sin], axis=2).reshape(seq, MLA_ROPE)
    ones = jnp.ones((seq, MLA_NOPE), F32)
    padc = jnp.ones((seq, MLA_SLOT - MLA_QK), F32)
    cos_full = jnp.concatenate([ones, cos_t, padc], axis=1)
    sin_full = jnp.concatenate([0 * ones, sin_t, 0 * padc], axis=1)
    return cos_full, sin_full


def _pack_stacked(p, off):
    w = p["w_in"].astype(BF16)
    depth, d = w.shape[0], w.shape[1]
    cols = lambda name, width: w[:, :, off[name]:off[name] + width]
    kr = cols("kr", MLA_ROPE)
    w_in = jnp.concatenate([
        cols("naq", NA_WIDTH), cols("nak", NA_WIDTH), cols("ckv", MLA_KV_RANK),
        kr, kr, jnp.zeros((depth, d, LANES - 2 * MLA_ROPE), BF16),
        cols("cq", MLA_Q_RANK), cols("gmu", GM_WIDTH), cols("gmv", GM_WIDTH), cols("gates", N_BRANCHES * d),
    ], axis=2)
    return {
        "w_in": w_in, "w_nav": cols("nav", NA_WIDTH),
        "w_na_o": p["na_w_o"].astype(BF16), "w_gm_o": p["gm_w_o"].astype(BF16),
        "w_mla_o": p["mla_w_o"].astype(BF16), "w_out": p["w_out"].astype(BF16),
        "w1": p["ffn_w1"].astype(BF16), "w2": p["ffn_w2"].astype(BF16),
    }


def _pack_layer(i, p, stacked):
    perm, tail = _rope_partner()
    q_scale = MLA_QK ** -0.5 * LOG2E
    w_uq = p["mla_w_uq"][i].reshape(MLA_Q_RANK, MLA_HEADS, MLA_QK)
    wq = _pad_heads(w_uq).astype(BF16)
    wqp = _pad_heads(w_uq[:, :, perm] * tail).astype(BF16)
    qg = p["mla_q_gain"][i]
    gq = jnp.pad(qg, (0, MLA_SLOT - MLA_QK))[None] * q_scale
    gqp = jnp.pad(qg[perm] * tail, (0, MLA_SLOT - MLA_QK))[None] * q_scale

    w_ukv = p["mla_w_ukv"][i].reshape(MLA_KV_RANK, MLA_HEADS, MLA_NOPE + MLA_V)
    k_nope = jnp.pad(w_ukv[:, :, :MLA_NOPE], ((0, 0), (0, 0), (0, MLA_SLOT - MLA_NOPE)))
    k_nope = k_nope.reshape(MLA_KV_RANK, MLA_HEADS * MLA_SLOT)
    sel = np.zeros((LANES, MLA_HEADS, MLA_SLOT), np.float32)
    selp = np.zeros((LANES, MLA_HEADS, MLA_SLOT), np.float32)
    for t in range(MLA_ROPE):
        for copy in range(2):
            sel[copy * MLA_ROPE + t, :, MLA_NOPE + t] = 1.0
            selp[copy * MLA_ROPE + t, :, perm[MLA_NOPE + t]] = 1.0
    wk = jnp.concatenate([k_nope, sel.reshape(LANES, -1)], axis=0).astype(BF16)
    wkp = jnp.concatenate([jnp.zeros_like(k_nope), selp.reshape(LANES, -1)], axis=0).astype(BF16)
    kg = p["mla_k_gain"][i]
    gk = jnp.pad(kg, (0, MLA_SLOT - MLA_QK))[None]
    gkp = jnp.pad(kg[perm] * tail, (0, MLA_SLOT - MLA_QK))[None]
    wvt = w_ukv[:, :, MLA_NOPE:].reshape(MLA_KV_RANK, MLA_HEADS * MLA_V).T.astype(BF16)

    blk = np.arange(2 * LANES) // NA_HEAD_DIM
    bd = jnp.asarray(blk[:, None] == blk[None, :], BF16)

    return {
        **stacked, "layer": i,
        "w_navt": stacked["w_nav"][i].T,
        "na_qg": jnp.tile(p["na_q_gain"][i], NA_HEADS)[None] * (NA_HEAD_DIM ** -0.5 * LOG2E),
        "na_kg": jnp.tile(p["na_k_gain"][i], NA_HEADS)[None],
        "bd": bd,
        "ckv_g": p["mla_ckv_gain"][i][None], "cq_g": p["mla_cq_gain"][i][None],
        "ln_g": p["gm_ln_g"][i][None], "ln_b": p["gm_ln_b"][i][None],
        "wq": wq, "wqp": wqp, "gq": gq, "gqp": gqp, "wk": wk, "wkp": wkp, "gk": gk, "gkp": gkp, "wvt": wvt,
        "w_s": p["gm_w_s"][i].astype(BF16),
        "b_s": jnp.repeat(p["gm_b_s"][i].T, GM_GROUP_DIM, axis=1),
    }


def _in_offsets(d):
    off = {"nak": 0}
    off["nav"] = off["nak"] + NA_WIDTH
    off["ckv"] = off["nav"] + NA_WIDTH
    off["kr"] = off["ckv"] + MLA_KV_RANK
    off["naq"] = off["kr"] + MLA_ROPE
    off["gmu"] = off["naq"] + NA_WIDTH
    off["gmv"] = off["gmu"] + GM_WIDTH
    off["cq"] = off["gmv"] + GM_WIDTH
    off["gates"] = off["cq"] + MLA_Q_RANK
    return off


def kernel(x, c, ctx, c_ctx, w_mod, b_mod, g_norm1, g_norm2, w_in, na_q_gain, na_k_gain, na_rpb, na_w_o, gm_ln_g,
           gm_ln_b, gm_w_s, gm_b_s, gm_w_o, mla_cq_gain, mla_ckv_gain, mla_w_uq, mla_w_ukv, mla_q_gain, mla_k_gain,
           mla_w_o, w_out, ffn_w1, ffn_w2):
    batch, seq, d = x.shape
    ctx_len = ctx.shape[1]
    depth = w_mod.shape[0]
    grid_rows = seq // GRID_W
    assert N_BRANCHES * d == C_TOTAL - C_GATES and d == w_out.shape[-1] and batch + 1 <= MOD_ROWS
    assert seq % (NA_QR * GRID_W) == 0 and grid_rows >= NA_BAND and (grid_rows - NA_BAND) % NA_QR == 0
    assert seq % min(ROW_TILE, seq) == 0 and seq % min(MLA_TQ, seq) == 0 and seq % min(MLA_TK, seq) == 0
    assert ctx_len % GM_CHUNK == 0 and (batch * ctx_len) % min(CTX_ROW_TILE, batch * ctx_len) == 0
    p = dict(w_in=w_in, na_q_gain=na_q_gain, na_k_gain=na_k_gain, na_w_o=na_w_o, gm_ln_g=gm_ln_g, gm_ln_b=gm_ln_b,
             gm_w_s=gm_w_s, gm_b_s=gm_b_s, gm_w_o=gm_w_o, mla_cq_gain=mla_cq_gain, mla_ckv_gain=mla_ckv_gain,
             mla_w_uq=mla_w_uq, mla_w_ukv=mla_w_ukv, mla_q_gain=mla_q_gain, mla_k_gain=mla_k_gain, mla_w_o=mla_w_o,
             w_out=w_out, ffn_w1=ffn_w1, ffn_w2=ffn_w2)
    stacked = _pack_stacked(p, _in_offsets(d))

    mod_rows = MOD_ROWS
    ctx_row = batch
    rows = jnp.concatenate([c, c_ctx[None], jnp.zeros((mod_rows - batch - 1, d), F32)], axis=0)
    mod = _modulation(rows, w_mod, b_mod).reshape(depth, mod_rows, 6, 1, d)

    tm = min(ROW_TILE, seq)
    tmc = min(CTX_ROW_TILE, batch * ctx_len)
    cos_t, sin_t = _rope_tables(seq)
    cos_c = jnp.ones((tmc, LANES), F32)
    sin_c = jnp.zeros((tmc, LANES), F32)
    tiles_per_seq = seq // tm
    tiles_per_ctx = 1
    lat_row = lambda i: i // tiles_per_seq
    ctx_row_fn = lambda i: ctx_row
    na_bias = _na_bias_tables(na_rpb, grid_rows)

    xs = x.reshape(batch * seq, d)
    cs = ctx.reshape(batch * ctx_len, d)
    for i in range(depth):
        last = i == depth - 1
        lw = _pack_layer(i, p, stacked)
        mod_l = mod[i]
        gn1 = g_norm1[i][None]
        gn2 = g_norm2[i][None]

        (naq, nak, navt, gmu, gmv, gates, qm, km, vmt) = _in_projection(
            xs, mod_l, lat_row, tm, tiles_per_seq, gn1, lw, cos_t, sin_t)
        (naq_c, nak_c, navt_c, gmu_c, gmv_c, gates_c, qm_c, km_c, vmt_c) = _in_projection(
            cs, mod_l, ctx_row_fn, tmc, tiles_per_ctx, gn1, lw, cos_c, sin_c)

        y_na = _na_attention(naq, nak, navt, nak_c, navt_c, na_bias, i, batch, seq, ctx_len)
        y_mla = _mla_attention(qm, km_c, vmt_c, km, vmt, batch, seq, ctx_len, seq, tq=min(MLA_TQ, seq),
                               tk=min(MLA_TK, seq))
        xs_new = _merge_ffn(xs, y_na, y_mla, gmu, gmv, gates, mod_l, lat_row, tm, gn2, lw)

        if not last:
            y_na_c = _na_ctx_attention(naq_c, nak_c, navt_c, batch, ctx_len)
            y_mla_c = _mla_attention(qm_c, km_c, vmt_c, None, None, batch, ctx_len, ctx_len, 0, tq=ctx_len, tk=0,
                                     pairs_per_step=MLA_HEADS // 2)
            cs = _merge_ffn(cs, y_na_c, y_mla_c, gmu_c, gmv_c, gates_c, mod_l, ctx_row_fn, tmc, gn2, lw)
        xs = xs_new
    return xs.reshape(batch, seq, d)
```

```python
import functools
import math

import numpy as np
import jax
import jax.numpy as jnp
from jax import lax
from jax.experimental import pallas as pl
from jax.experimental.pallas import tpu as pltpu

F32 = jnp.float32
BF16 = jnp.bfloat16

LANES = 128
BF16_SUBLANES = 16
VMEM_LIMIT_BYTES = 60 * 1024 * 1024

GRID_W = 64
N_BRANCHES = 3
NA_HEADS = 8
NA_HEAD_DIM = 64
NA_WIDTH = NA_HEADS * NA_HEAD_DIM
NA_WIN_H = 8
NA_WIN_W = 16
GM_GROUPS = 4
GM_CHUNK = 128
GM_GROUP_DIM = 128
GM_WIDTH = GM_GROUPS * GM_GROUP_DIM
MLA_HEADS = 8
MLA_Q_RANK = 256
MLA_KV_RANK = 128
MLA_NOPE = 64
MLA_ROPE = 32
MLA_V = 64
MLA_QK = MLA_NOPE + MLA_ROPE
ROPE_AXIS_PAIRS = MLA_ROPE // 4
ROPE_THETA = 10000.0
NORM_EPS = 1e-6
NEG_INF = -1e30
LOG2E = math.log2(math.e)

NA_QR = 4
NA_BAND = 12
NA_BLOCKS_PER_STEP = 4
MLA_SLOT = LANES
ROW_TILE = 512
CTX_ROW_TILE = 512
MLA_TQ = 2048
MLA_TK = 256
MLA_CHAIN_Q = 256
GATE_CHUNK = 512
MOD_COL_TILE = 1536
MOD_ROWS = 16


def _resident(shape):
    zeros = (0,) * len(shape)
    return pl.BlockSpec(shape, lambda *_: zeros, pipeline_mode=pl.Buffered(1))


def _resident_layer(shape, layer):
    zeros = (0,) * len(shape)
    return pl.BlockSpec((None,) + tuple(shape), lambda *_: (layer,) + zeros, pipeline_mode=pl.Buffered(1))


def _params():
    return pltpu.CompilerParams(vmem_limit_bytes=VMEM_LIMIT_BYTES)


def _dot(a, b):
    return jnp.dot(a, b, preferred_element_type=F32)


def _dot_t(a, b):
    return lax.dot_general(a, b, (((1,), (1,)), ((), ())), preferred_element_type=F32)


def _mod_kernel(s_ref, w_ref, b_ref, o_ref):
    s = s_ref[...]
    s = s * jax.nn.sigmoid(s)
    o_ref[...] = _dot(s.astype(BF16), w_ref[...].astype(BF16)) + b_ref[...]


def _modulation(rows, w_mod, b_mod):
    depth, d, n = w_mod.shape
    r = rows.shape[0]
    tn = MOD_COL_TILE
    return pl.pallas_call(
        _mod_kernel,
        grid=(depth, n // tn),
        in_specs=[
            pl.BlockSpec((r, d), lambda l, j: (0, 0)),
            pl.BlockSpec((None, d, tn), lambda l, j: (l, 0, j)),
            pl.BlockSpec((None, 1, tn), lambda l, j: (l, 0, j)),
        ],
        out_specs=pl.BlockSpec((None, r, tn), lambda l, j: (l, 0, j)),
        out_shape=jax.ShapeDtypeStruct((depth, r, n), F32),
        compiler_params=_params(),
        name="modulation",
    )(rows, w_mod, b_mod.reshape(depth, 1, n))


C_NAQ = 0
C_NAK = C_NAQ + NA_WIDTH
C_CKV = C_NAK + NA_WIDTH
C_KR = C_CKV + MLA_KV_RANK
C_CQ = C_KR + LANES
C_GMU = C_CQ + MLA_Q_RANK
C_GMV = C_GMU + GM_WIDTH
C_GATES = C_GMV + GM_WIDTH
C_TOTAL = C_GATES + N_BRANCHES * 1024


def _rms(t, width):
    return t * lax.rsqrt(jnp.sum(t * t, axis=-1, keepdims=True) * (1.0 / width) + NORM_EPS)


def _gelu(t):
    return 0.5 * t * (1.0 + lax.erf(t * math.sqrt(0.5)))


def _sigmoid(t):
    return 0.5 * jnp.tanh(0.5 * t) + 0.5


def _head64_norm(p, bd):
    p2 = (p * p).astype(BF16)
    half = 2 * LANES
    ss = jnp.concatenate([_dot(p2[:, :half], bd), _dot(p2[:, half:], bd)], axis=1)
    return p * lax.rsqrt(ss * (1.0 / NA_HEAD_DIM) + NORM_EPS)


def _inproj_kernel(x_ref, sh_ref, sc_ref, gn_ref, w_ref, wnavt_ref, qg_ref, kg_ref, bd_ref, ckvg_ref, cqg_ref, lng_ref,
                   lnb_ref, wq_ref, wqp_ref, gq_ref, gqp_ref, wk_ref, wkp_ref, gk_ref, gkp_ref, wvt_ref, cos_ref,
                   sin_ref, naq_ref, nak_ref, navt_ref, gmu_ref, gmv_ref, gates_ref, qm_ref, km_ref, vmt_ref):
    x = x_ref[...]
    d = x.shape[-1]
    h = _rms(x, d) * gn_ref[...]
    h = h * (1.0 + sc_ref[...]) + sh_ref[...]
    hb = h.astype(BF16)

    def proj(c0, width):
        return _dot(hb, w_ref[:, c0:c0 + width])

    p_ckv = proj(C_CKV, MLA_KV_RANK)
    p_cq = proj(C_CQ, MLA_Q_RANK)
    kr = proj(C_KR, LANES)
    p_naq = proj(C_NAQ, NA_WIDTH)
    p_nak = proj(C_NAK, NA_WIDTH)
    navt_ref[...] = _dot_t(wnavt_ref[...], hb).astype(BF16)

    ckv_n = (_rms(p_ckv, MLA_KV_RANK) * ckvg_ref[...]).astype(BF16)
    cq_n = (_rms(p_cq, MLA_Q_RANK) * cqg_ref[...]).astype(BF16)
    kr_hi = kr.astype(BF16)
    kr_lo = (kr - kr_hi.astype(F32)).astype(BF16)
    lane = lax.broadcasted_iota(jnp.int32, (1, LANES), 1)
    kr_hl = jnp.where(lane < MLA_ROPE, kr_hi, kr_lo)
    lhs_k = jnp.concatenate([ckv_n, kr_hl], axis=1)

    p_gmu = proj(C_GMU, GM_WIDTH)
    p_gmv = proj(C_GMV, GM_WIDTH)

    bd = bd_ref[...]
    naq_ref[...] = (_head64_norm(p_naq, bd) * qg_ref[...]).astype(BF16)
    nak_ref[...] = (_head64_norm(p_nak, bd) * kg_ref[...]).astype(BF16)

    vmt_ref[...] = _dot_t(wvt_ref[...], ckv_n).astype(BF16)
    xq, xqp = _dot(cq_n, wq_ref[...]), _dot(cq_n, wqp_ref[...])
    xk, xkp = _dot(lhs_k, wk_ref[...]), _dot(lhs_k, wkp_ref[...])

    gmu_ref[...] = _gelu(p_gmu).astype(BF16)
    gv = _gelu(p_gmv)
    mu = jnp.mean(gv, axis=-1, keepdims=True)
    gc = gv - mu
    var = jnp.mean(gc * gc, axis=-1, keepdims=True)
    gmv_ref[...] = (gc * lax.rsqrt(var + NORM_EPS) * lng_ref[...] + lnb_ref[...]).astype(BF16)

    n_gate = gates_ref.shape[-1]
    gate_chunk = GATE_CHUNK
    p_gate = proj(C_GATES, gate_chunk)

    cosv = cos_ref[...]
    sinv = sin_ref[...]

    def heads(xa, xpa, g_r, gp_r, out_r):
        gcos, gsin = g_r[...] * cosv, gp_r[...] * sinv
        for hd in range(MLA_HEADS):
            sl = slice(hd * MLA_SLOT, (hd + 1) * MLA_SLOT)
            xh = xa[:, sl]
            r = lax.rsqrt(jnp.sum(xh * xh, axis=-1, keepdims=True) * (1.0 / MLA_QK) + NORM_EPS)
            out_r[:, sl] = (r * (xh * gcos + xpa[:, sl] * gsin)).astype(BF16)

    heads(xq, xqp, gq_ref, gqp_ref, qm_ref)
    heads(xk, xkp, gk_ref, gkp_ref, km_ref)

    for j in range(n_gate // gate_chunk):
        nxt = proj(C_GATES + (j + 1) * gate_chunk, gate_chunk) if (j + 1) * gate_chunk < n_gate else None
        gates_ref[:, j * gate_chunk:(j + 1) * gate_chunk] = _sigmoid(p_gate).astype(BF16)
        p_gate = nxt


def _in_projection(xs, mod_l, mod_row_fn, tm, tiles_per_seq, gn, lw, cos_t, sin_t):
    r, d = xs.shape
    n_tiles = r // tm
    row = lambda w: pl.BlockSpec((tm, w), lambda i: (i, 0))
    modspec = lambda chunk: pl.BlockSpec((None, None, 1, d), lambda i: (mod_row_fn(i), chunk, 0, 0))
    pos = pl.BlockSpec((tm, LANES), lambda i: (i % tiles_per_seq, 0))
    in_specs = [
        row(d), modspec(0), modspec(1), _resident((1, d)), _resident_layer((d, C_TOTAL), lw["layer"]),
        _resident((NA_WIDTH, d)),
        _resident((1, NA_WIDTH)), _resident((1, NA_WIDTH)), _resident((2 * LANES, 2 * LANES)),
        _resident((1, MLA_KV_RANK)), _resident((1, MLA_Q_RANK)), _resident((1, GM_WIDTH)), _resident((1, GM_WIDTH)),
        _resident((MLA_Q_RANK, MLA_HEADS * MLA_SLOT)), _resident((MLA_Q_RANK, MLA_HEADS * MLA_SLOT)),
        _resident((1, MLA_SLOT)), _resident((1, MLA_SLOT)),
        _resident((2 * LANES, MLA_HEADS * MLA_SLOT)), _resident((2 * LANES, MLA_HEADS * MLA_SLOT)),
        _resident((1, MLA_SLOT)), _resident((1, MLA_SLOT)),
        _resident((MLA_HEADS * MLA_V, MLA_KV_RANK)), pos, pos,
    ]
    outs = [("naq", NA_WIDTH, False), ("nak", NA_WIDTH, False), ("navt", NA_WIDTH, True), ("gmu", GM_WIDTH, False),
            ("gmv", GM_WIDTH, False), ("gates", N_BRANCHES * d, False), ("qm", MLA_HEADS * MLA_SLOT, False),
            ("km", MLA_HEADS * MLA_SLOT, False), ("vmt", MLA_HEADS * MLA_V, True)]
    col = lambda w: pl.BlockSpec((w, tm), lambda i: (0, i))
    return pl.pallas_call(
        _inproj_kernel,
        grid=(n_tiles,),
        in_specs=in_specs,
        out_specs=[col(w) if t else row(w) for _, w, t in outs],
        out_shape=[jax.ShapeDtypeStruct((w, r) if t else (r, w), BF16) for _, w, t in outs],
        compiler_params=_params(),
        name="in_projection",
    )(xs, mod_l, mod_l, gn, lw["w_in"], lw["w_navt"], lw["na_qg"], lw["na_kg"], lw["bd"], lw["ckv_g"], lw["cq_g"],
      lw["ln_g"],
      lw["ln_b"], lw["wq"], lw["wqp"], lw["gq"], lw["gqp"], lw["wk"], lw["wkp"], lw["gk"], lw["gkp"], lw["wvt"],
      cos_t, sin_t)


def _pair_scores(q_pair, k_blocks):
    lane = lax.broadcasted_iota(jnp.int32, (1, LANES), 1)
    zero = jnp.zeros_like(q_pair)
    q2 = jnp.concatenate([jnp.where(lane < NA_HEAD_DIM, q_pair, zero), jnp.where(lane >= NA_HEAD_DIM, q_pair, zero)],
                         axis=0)
    return [_dot_t(k_pair, q2) if bias_t is None else _dot_t(k_pair, q2) + bias_t for k_pair, bias_t in k_blocks]


def _pair_softmax_pv(sts, vt_blocks):
    nq = sts[0].shape[1] // 2
    m = None
    for st in sts:
        mb = jnp.max(st, axis=0, keepdims=True)
        m = mb if m is None else jnp.maximum(m, mb)
    acc = None
    for st, vt_pair in zip(sts, vt_blocks):
        ones = jnp.ones((BF16_SUBLANES, vt_pair.shape[1]), BF16)
        part = _dot(jnp.concatenate([vt_pair, ones], axis=0), jnp.exp2(st - m).astype(BF16))
        acc = part if acc is None else acc + part
    o0 = acc[:NA_HEAD_DIM, :nq] / acc[LANES:LANES + 1, :nq]
    o1 = acc[NA_HEAD_DIM:LANES, nq:] / acc[LANES:LANES + 1, nq:]
    return jnp.concatenate([o0, o1], axis=0).T


def _pairs_pipelined(n_pairs, scores_fn, finish_fn):
    all_scores = [scores_fn(p) for p in range(n_pairs)]
    for p in range(n_pairs):
        finish_fn(p, all_scores[p])


def _na_kernel(q_ref, k_ref, vt_ref, kc_ref, vct_ref, bias_ref, o_ref, *, grid_rows, blocks_per_step):
    n_rb = grid_rows // NA_QR
    nk = NA_BAND * GRID_W
    tq = NA_QR * GRID_W
    n_pairs = NA_HEADS // 2
    cols = lambda hp: slice(hp * LANES, (hp + 1) * LANES)

    def geometry(sub):
        rb = pl.program_id(1) * blocks_per_step + sub
        cls = jnp.where(rb == 0, 0, jnp.where(rb == n_rb - 1, 2, 1))
        band0 = jnp.clip(NA_QR * rb - NA_WIN_H // 2, 0, grid_rows - NA_BAND)
        return cls, pl.multiple_of(band0 * GRID_W, 2 * LANES)

    geo = [geometry(sub) for sub in range(blocks_per_step)]

    def scores(c):
        sub, hp = divmod(c, n_pairs)
        cls, k0 = geo[sub]
        cs = cols(hp)
        k_all = jnp.concatenate([k_ref[pl.ds(k0, nk), cs], kc_ref[:, cs]], axis=0)
        st = _pair_scores(q_ref[sub * tq:(sub + 1) * tq, cs], [(k_all, None)])[0]
        return [jnp.concatenate([st[:nk] + bias_ref[cls, hp], st[nk:]], axis=0)]

    def finish(c, sts):
        sub, hp = divmod(c, n_pairs)
        _, k0 = geo[sub]
        cs = cols(hp)
        vt_all = jnp.concatenate([vt_ref[cs, pl.ds(k0, nk)], vct_ref[cs, :]], axis=1)
        o_ref[sub * tq:(sub + 1) * tq, cs] = _pair_softmax_pv(sts, [vt_all]).astype(BF16)

    nxt = [scores(p) for p in range(n_pairs)]
    for sub in range(blocks_per_step):
        cur = nxt
        if sub + 1 < blocks_per_step:
            nxt = [scores((sub + 1) * n_pairs + p) for p in range(n_pairs)]
        for p in range(n_pairs):
            finish(sub * n_pairs + p, cur[p])


def _na_attention(q, k, vt, kc, vct, bias, layer, batch, seq, ctx_len):
    grid_rows = seq // GRID_W
    blocks_per_step = math.gcd(NA_BLOCKS_PER_STEP, grid_rows // NA_QR)
    n_rb = grid_rows // NA_QR // blocks_per_step
    tq = NA_QR * GRID_W * blocks_per_step
    nk = NA_BAND * GRID_W

    return pl.pallas_call(
        functools.partial(_na_kernel, grid_rows=grid_rows, blocks_per_step=blocks_per_step),
        grid=(batch, n_rb),
        in_specs=[
            pl.BlockSpec((tq, NA_WIDTH), lambda b, rb: (b * n_rb + rb, 0)),
            pl.BlockSpec((seq, NA_WIDTH), lambda b, rb: (b, 0)),
            pl.BlockSpec((NA_WIDTH, seq), lambda b, rb: (0, b)),
            pl.BlockSpec((ctx_len, NA_WIDTH), lambda b, rb: (b, 0)),
            pl.BlockSpec((NA_WIDTH, ctx_len), lambda b, rb: (0, b)),
            pl.BlockSpec((None, 3, NA_HEADS // 2, nk, 2 * NA_QR * GRID_W), lambda b, rb: (layer, 0, 0, 0, 0),
                         pipeline_mode=pl.Buffered(1)),
        ],
        out_specs=pl.BlockSpec((tq, NA_WIDTH), lambda b, rb: (b * n_rb + rb, 0)),
        out_shape=jax.ShapeDtypeStruct((batch * seq, NA_WIDTH), BF16),
        compiler_params=_params(),
        name="na_attention",
    )(q, k, vt, kc, vct, bias)


NA_DR = 2 * NA_WIN_H - 1
NA_DC = 2 * NA_WIN_W - 1


def _na_window_geometry(grid_rows):
    n_rb = grid_rows // NA_QR
    kh, kw = NA_WIN_H, NA_WIN_W
    cq = np.arange(GRID_W)
    c0 = np.clip(cq - kw // 2, 0, GRID_W - kw)
    col_in = (cq[None, :] >= c0[:, None]) & (cq[None, :] < c0[:, None] + kw)
    dc = np.clip(cq[None, :] - cq[:, None], -(kw - 1), kw - 1) + kw - 1
    dc_t = np.tile(dc.T, (1, 2)).astype(np.int32)
    col_in_t = np.tile(col_in.T, (1, 2)).astype(np.int32)
    dr = np.zeros((3, NA_QR, NA_BAND), np.int32)
    ok = np.zeros((3, NA_QR, NA_BAND), bool)
    for c, rb in enumerate((0, 1, n_rb - 1)):
        band0 = int(np.clip(NA_QR * rb - kh // 2, 0, grid_rows - NA_BAND))
        for i in range(NA_QR):
            rq = NA_QR * rb + i
            r0 = int(np.clip(rq - kh // 2, 0, grid_rows - kh))
            for j in range(NA_BAND):
                kr = band0 + j
                ok[c, i, j] = r0 <= kr < r0 + kh
                dr[c, i, j] = int(np.clip(kr - rq + kh - 1, 0, NA_DR - 1))
    return dc_t, col_in_t, dr, ok


def _na_bias_kernel(rpb_ref, dc_ref, colin_ref, o_ref, tile_ref, *, dr, ok):
    layer = pl.program_id(0)
    dc = dc_ref[...]
    col_in = colin_ref[...] > 0
    n_tiles = NA_HEADS * NA_DR

    def build(t, carry):
        base = (layer * n_tiles + t) * NA_DC
        acc = jnp.zeros(dc.shape, F32)
        for d in range(NA_DC):
            acc = jnp.where(dc == d, rpb_ref[base + d], acc)
        tile_ref[t] = jnp.where(col_in, acc * LOG2E, NEG_INF)
        return carry

    lax.fori_loop(0, n_tiles, build, 0)

    lane = lax.broadcasted_iota(jnp.int32, (1, LANES), 1)
    neg = jnp.full((GRID_W, LANES), NEG_INF, F32)
    for c in range(3):
        for hp in range(NA_HEADS // 2):
            for j in range(NA_BAND):
                for g in range(2 * NA_QR // 2):
                    head = 2 * hp + g // (NA_QR // 2)
                    i0 = 2 * (g % (NA_QR // 2))
                    halves = [tile_ref[head * NA_DR + int(dr[c, i, j])] if ok[c, i, j] else neg for i in (i0, i0 + 1)]
                    o_ref[c, hp, j * GRID_W:(j + 1) * GRID_W, g * LANES:(g + 1) * LANES] = jnp.where(
                        lane < GRID_W, halves[0], halves[1])


def _na_bias_tables(na_rpb, grid_rows):
    depth = na_rpb.shape[0]
    dc_t, col_in_t, dr, ok = _na_window_geometry(grid_rows)
    nk, nq2 = NA_BAND * GRID_W, 2 * NA_QR * GRID_W
    return pl.pallas_call(
        functools.partial(_na_bias_kernel, dr=dr, ok=ok),
        grid=(depth,),
        in_specs=[
            pl.BlockSpec(memory_space=pltpu.SMEM),
            pl.BlockSpec((GRID_W, LANES), lambda l: (0, 0)),
            pl.BlockSpec((GRID_W, LANES), lambda l: (0, 0)),
        ],
        out_specs=pl.BlockSpec((None, 3, NA_HEADS // 2, nk, nq2), lambda l: (l, 0, 0, 0, 0)),
        out_shape=jax.ShapeDtypeStruct((depth, 3, NA_HEADS // 2, nk, nq2), F32),
        scratch_shapes=[pltpu.VMEM((NA_HEADS * NA_DR, GRID_W, LANES), F32)],
        compiler_params=_params(),
        name="na_bias_tables",
    )(na_rpb.reshape(-1), jnp.asarray(dc_t), jnp.asarray(col_in_t))


def _mla_kernel(q_ref, kc_ref, vct_ref, *rest, n_kb, tk):
    if n_kb:
        k_ref, vt_ref, o_ref = rest
    else:
        (o_ref,) = rest
    tq = q_ref.shape[0]
    pairs = q_ref.shape[1] // (2 * MLA_SLOT)
    slot = lambda pp, h: slice((2 * pp + h) * MLA_SLOT, (2 * pp + h + 1) * MLA_SLOT)
    pair_rows = lambda pp: slice(pp * 2 * MLA_V, (pp + 1) * 2 * MLA_V)
    lo = lax.broadcasted_iota(jnp.int32, (2 * MLA_V, 1), 0) < MLA_V

    def values(vt_blk):
        one = jnp.ones_like(vt_blk)
        return jnp.where(lo, vt_blk, one), jnp.where(lo, one, vt_blk)

    def update(st, m, acc, vt_aug):
        m_new = jnp.maximum(m, jnp.max(st, axis=0, keepdims=True))
        return m_new, jnp.exp2(m - m_new) * acc + _dot(vt_aug, jnp.exp2(st - m_new).astype(BF16))

    k_blk = lambda j, cs: kc_ref[:, cs] if j < 0 else k_ref[pl.ds(j * tk, tk), cs]
    vt_blk = lambda j, rs: vct_ref[rs, :] if j < 0 else vt_ref[rs, pl.ds(j * tk, tk)]
    blocks = list(range(-1, n_kb))
    nq = min(MLA_CHAIN_Q, tq)
    chains = [(pp, h, s) for pp in range(pairs) for s in range(tq // nq) for h in range(2)]
    q = [q_ref[s * nq:(s + 1) * nq, slot(pp, h)] for pp, h, s in chains]
    scores = lambda j, c: _dot_t(k_blk(j, slot(*chains[c][:2])), q[c])
    nxt = [scores(blocks[0], c) for c in range(len(chains))]
    m = [jnp.full((1, nq), NEG_INF, F32)] * len(chains)
    acc = [jnp.zeros((2 * MLA_V, nq), F32)] * len(chains)
    for idx, j in enumerate(blocks):
        vt_aug = [values(vt_blk(j, pair_rows(pp))) for pp in range(pairs)]
        for c, (pp, h, _) in enumerate(chains):
            st = nxt[c]
            if idx + 1 < len(blocks):
                nxt[c] = scores(blocks[idx + 1], c)
            m[c], acc[c] = update(st, m[c], acc[c], vt_aug[pp][h])
    for pp in range(pairs):
        acc_h = [jnp.concatenate([acc[c] for c, ch in enumerate(chains) if ch[:2] == (pp, hh)], axis=1)
                 for hh in range(2)]
        o0 = acc_h[0][:MLA_V] / acc_h[0][MLA_V:]
        o1 = acc_h[1][MLA_V:] / acc_h[1][:MLA_V]
        o_ref[:, pp * LANES:(pp + 1) * LANES] = jnp.concatenate([o0, o1], axis=0).T.astype(BF16)


def _mla_attention(q, kc, vct, k, vt, batch, q_len, ctx_len, kv_len, tq, tk, pairs_per_step=1):
    n_q = q_len // tq
    n_pairs = MLA_HEADS // 2 // pairs_per_step
    n_kb = kv_len // tk if k is not None else 0
    qk_w, v_w = pairs_per_step * 2 * MLA_SLOT, pairs_per_step * 2 * MLA_V
    in_specs = [
        pl.BlockSpec((tq, qk_w), lambda b, p, i: (b * n_q + i, p)),
        pl.BlockSpec((ctx_len, qk_w), lambda b, p, i: (b, p)),
        pl.BlockSpec((v_w, ctx_len), lambda b, p, i: (p, b)),
    ]
    args = [q, kc, vct]
    if n_kb:
        in_specs += [
            pl.BlockSpec((kv_len, qk_w), lambda b, p, i: (b, p)),
            pl.BlockSpec((v_w, kv_len), lambda b, p, i: (p, b)),
        ]
        args += [k, vt]
    return pl.pallas_call(
        functools.partial(_mla_kernel, n_kb=n_kb, tk=tk),
        grid=(batch, n_pairs, n_q),
        in_specs=in_specs,
        out_specs=pl.BlockSpec((tq, v_w), lambda b, p, i: (b * n_q + i, p)),
        out_shape=jax.ShapeDtypeStruct((batch * q_len, MLA_HEADS * MLA_V), BF16),
        compiler_params=_params(),
        name="mla_attention" if n_kb else "mla_ctx_attention",
    )(*args)


def _na_ctx_kernel(q_ref, k_ref, vt_ref, o_ref):
    cols = lambda hp: slice(hp * LANES, (hp + 1) * LANES)

    def scores(hp):
        return _pair_scores(q_ref[:, cols(hp)], [(k_ref[:, cols(hp)], None)])

    def finish(hp, sts):
        o_ref[:, cols(hp)] = _pair_softmax_pv(sts, [vt_ref[cols(hp), :]]).astype(BF16)

    _pairs_pipelined(NA_HEADS // 2, scores, finish)


def _na_ctx_attention(q, k, vt, batch, ctx_len):
    spec = pl.BlockSpec((ctx_len, NA_WIDTH), lambda b: (b, 0))
    return pl.pallas_call(
        _na_ctx_kernel,
        grid=(batch,),
        in_specs=[spec, spec, pl.BlockSpec((NA_WIDTH, ctx_len), lambda b: (0, b))],
        out_specs=spec,
        out_shape=jax.ShapeDtypeStruct((batch * ctx_len, NA_WIDTH), BF16),
        compiler_params=_params(),
        name="na_ctx_attention",
    )(q, k, vt)


FFN_HIDDEN_CHUNK = 512


def _merge_ffn_kernel(x_ref, yna_ref, ymla_ref, gmu_ref, gmv_ref, gates_ref, gt1_ref, sh2_ref, sc2_ref, gt2_ref,
                      gn2_ref, wna_ref, wgm_ref, wmla_ref, wout_ref, w1_ref, w2_ref, ws_ref, bsb_ref, o_ref,
                      ygm_ref):
    tm, d = x_ref.shape
    for c in range(tm // GM_CHUNK):
        rs = slice(c * GM_CHUNK, (c + 1) * GM_CHUNK)
        for g in range(GM_GROUPS):
            cs = slice(g * GM_GROUP_DIM, (g + 1) * GM_GROUP_DIM)
            mixed = _dot(ws_ref[g], gmv_ref[rs, cs]) + bsb_ref[:, cs]
            ygm_ref[rs, cs] = (gmu_ref[rs, cs].astype(F32) * mixed).astype(BF16)

    y = gates_ref[:, 0:d].astype(F32) * _dot(yna_ref[...], wna_ref[...])
    y = y + gates_ref[:, d:2 * d].astype(F32) * _dot(ygm_ref[...], wgm_ref[...])
    y = y + gates_ref[:, 2 * d:3 * d].astype(F32) * _dot(ymla_ref[...], wmla_ref[...])
    xn = x_ref[...] + gt1_ref[...] * _dot(y.astype(BF16), wout_ref[...])

    h2 = _rms(xn, d) * gn2_ref[...]
    h2 = (h2 * (1.0 + sc2_ref[...]) + sh2_ref[...]).astype(BF16)
    hidden = w1_ref.shape[1]
    acc = None
    for c in range(hidden // FFN_HIDDEN_CHUNK):
        cs = slice(c * FFN_HIDDEN_CHUNK, (c + 1) * FFN_HIDDEN_CHUNK)
        a = jnp.maximum(_dot(h2, w1_ref[:, cs]), 0.0)
        part = _dot((a * a).astype(BF16), w2_ref[cs, :])
        acc = part if acc is None else acc + part
    o_ref[...] = xn + gt2_ref[...] * acc


def _merge_ffn(xs, y_na, y_mla, gm_u, gm_v, gates, mod_l, mod_row_fn, tm, gn2, lw):
    r, d = xs.shape
    hidden = lw["w1"].shape[-1]
    wl = lambda shape: _resident_layer(shape, lw["layer"])
    row = lambda w: pl.BlockSpec((tm, w), lambda i: (i, 0))
    modspec = lambda chunk: pl.BlockSpec((None, None, 1, d), lambda i: (mod_row_fn(i), chunk, 0, 0))
    in_specs = [
        row(d), row(NA_WIDTH), row(MLA_HEADS * MLA_V), row(GM_WIDTH), row(GM_WIDTH), row(N_BRANCHES * d),
        modspec(2), modspec(3), modspec(4), modspec(5), _resident((1, d)),
        wl((NA_WIDTH, d)), wl((GM_WIDTH, d)), wl((MLA_HEADS * MLA_V, d)), wl((d, d)),
        wl((d, hidden)), wl((hidden, d)), _resident((GM_GROUPS, GM_CHUNK, GM_CHUNK)),
        _resident((GM_CHUNK, GM_WIDTH)),
    ]
    return pl.pallas_call(
        _merge_ffn_kernel,
        grid=(r // tm,),
        in_specs=in_specs,
        out_specs=row(d),
        out_shape=jax.ShapeDtypeStruct((r, d), F32),
        scratch_shapes=[pltpu.VMEM((tm, GM_WIDTH), BF16)],
        compiler_params=_params(),
        name="merge_ffn",
    )(xs, y_na, y_mla, gm_u, gm_v, gates, mod_l, mod_l, mod_l, mod_l, gn2, lw["w_na_o"], lw["w_gm_o"],
      lw["w_mla_o"], lw["w_out"], lw["w1"], lw["w2"], lw["w_s"], lw["b_s"])


def _rope_partner():
    perm = np.arange(MLA_QK)
    for u in range(MLA_ROPE):
        seg, half, pair = u // (2 * ROPE_AXIS_PAIRS), (u // ROPE_AXIS_PAIRS) % 2, u % ROPE_AXIS_PAIRS
        perm[MLA_NOPE + u] = MLA_NOPE + seg * 2 * ROPE_AXIS_PAIRS + (1 - half) * ROPE_AXIS_PAIRS + pair
    tail = np.arange(MLA_QK) >= MLA_NOPE
    return perm, tail


def _pad_heads(t):
    pad = [(0, 0)] * (t.ndim - 1) + [(0, MLA_SLOT - MLA_QK)]
    t = jnp.pad(t, pad)
    return t.reshape(t.shape[:-2] + (MLA_HEADS * MLA_SLOT,))


def _rope_tables(seq):
    t = jnp.arange(seq)
    rows = (t // GRID_W).astype(F32)
    colsv = (t % GRID_W).astype(F32)
    freqs = ROPE_THETA ** (-jnp.arange(ROPE_AXIS_PAIRS, dtype=F32) / ROPE_AXIS_PAIRS)
    ang = jnp.stack([rows[:, None] * freqs, colsv[:, None] * freqs], axis=1)
    cos, sin = jnp.cos(ang), jnp.sin(ang)
    cos_t = jnp.stack([cos, cos], axis=2).reshape(seq, MLA_ROPE)
    sin_t = jnp.stack([-sin, sin], axis=2).reshape(seq, MLA_ROPE)
    ones = jnp.ones((seq, MLA_NOPE), F32)
    padc = jnp.ones((seq, MLA_SLOT - MLA_QK), F32)
    cos_full = jnp.concatenate([ones, cos_t, padc], axis=1)
    sin_full = jnp.concatenate([0 * ones, sin_t, 0 * padc], axis=1)
    return cos_full, sin_full


def _pack_stacked(p, off):
    w = p["w_in"].astype(BF16)
    depth, d = w.shape[0], w.shape[1]
    cols = lambda name, width: w[:, :, off[name]:off[name] + width]
    kr = cols("kr", MLA_ROPE)
    w_in = jnp.concatenate([
        cols("naq", NA_WIDTH), cols("nak", NA_WIDTH), cols("ckv", MLA_KV_RANK),
        kr, kr, jnp.zeros((depth, d, LANES - 2 * MLA_ROPE), BF16),
        cols("cq", MLA_Q_RANK), cols("gmu", GM_WIDTH), cols("gmv", GM_WIDTH), cols("gates", N_BRANCHES * d),
    ], axis=2)
    return {
        "w_in": w_in, "w_nav": cols("nav", NA_WIDTH),
        "w_na_o": p["na_w_o"].astype(BF16), "w_gm_o": p["gm_w_o"].astype(BF16),
        "w_mla_o": p["mla_w_o"].astype(BF16), "w_out": p["w_out"].astype(BF16),
        "w1": p["ffn_w1"].astype(BF16), "w2": p["ffn_w2"].astype(BF16),
    }


def _pack_layer(i, p, stacked):
    perm, tail = _rope_partner()
    q_scale = MLA_QK ** -0.5 * LOG2E
    w_uq = p["mla_w_uq"][i].reshape(MLA_Q_RANK, MLA_HEADS, MLA_QK)
    wq = _pad_heads(w_uq).astype(BF16)
    wqp = _pad_heads(w_uq[:, :, perm] * tail).astype(BF16)
    qg = p["mla_q_gain"][i]
    gq = jnp.pad(qg, (0, MLA_SLOT - MLA_QK))[None] * q_scale
    gqp = jnp.pad(qg[perm] * tail, (0, MLA_SLOT - MLA_QK))[None] * q_scale

    w_ukv = p["mla_w_ukv"][i].reshape(MLA_KV_RANK, MLA_HEADS, MLA_NOPE + MLA_V)
    k_nope = jnp.pad(w_ukv[:, :, :MLA_NOPE], ((0, 0), (0, 0), (0, MLA_SLOT - MLA_NOPE)))
    k_nope = k_nope.reshape(MLA_KV_RANK, MLA_HEADS * MLA_SLOT)
    sel = np.zeros((LANES, MLA_HEADS, MLA_SLOT), np.float32)
    selp = np.zeros((LANES, MLA_HEADS, MLA_SLOT), np.float32)
    for t in range(MLA_ROPE):
        for copy in range(2):
            sel[copy * MLA_ROPE + t, :, MLA_NOPE + t] = 1.0
            selp[copy * MLA_ROPE + t, :, perm[MLA_NOPE + t]] = 1.0
    wk = jnp.concatenate([k_nope, sel.reshape(LANES, -1)], axis=0).astype(BF16)
    wkp = jnp.concatenate([jnp.zeros_like(k_nope), selp.reshape(LANES, -1)], axis=0).astype(BF16)
    kg = p["mla_k_gain"][i]
    gk = jnp.pad(kg, (0, MLA_SLOT - MLA_QK))[None]
    gkp = jnp.pad(kg[perm] * tail, (0, MLA_SLOT - MLA_QK))[None]
    wvt = w_ukv[:, :, MLA_NOPE:].reshape(MLA_KV_RANK, MLA_HEADS * MLA_V).T.astype(BF16)

    blk = np.arange(2 * LANES) // NA_HEAD_DIM
    bd = jnp.asarray(blk[:, None] == blk[None, :], BF16)

    return {
        **stacked, "layer": i,
        "w_navt": stacked["w_nav"][i].T,
        "na_qg": jnp.tile(p["na_q_gain"][i], NA_HEADS)[None] * (NA_HEAD_DIM ** -0.5 * LOG2E),
        "na_kg": jnp.tile(p["na_k_gain"][i], NA_HEADS)[None],
        "bd": bd,
        "ckv_g": p["mla_ckv_gain"][i][None], "cq_g": p["mla_cq_gain"][i][None],
        "ln_g": p["gm_ln_g"][i][None], "ln_b": p["gm_ln_b"][i][None],
        "wq": wq, "wqp": wqp, "gq": gq, "gqp": gqp, "wk": wk, "wkp": wkp, "gk": gk, "gkp": gkp, "wvt": wvt,
        "w_s": p["gm_w_s"][i].astype(BF16),
        "b_s": jnp.repeat(p["gm_b_s"][i].T, GM_GROUP_DIM, axis=1),
    }


def _in_offsets(d):
    off = {"nak": 0}
    off["nav"] = off["nak"] + NA_WIDTH
    off["ckv"] = off["nav"] + NA_WIDTH
    off["kr"] = off["ckv"] + MLA_KV_RANK
    off["naq"] = off["kr"] + MLA_ROPE
    off["gmu"] = off["naq"] + NA_WIDTH
    off["gmv"] = off["gmu"] + GM_WIDTH
    off["cq"] = off["gmv"] + GM_WIDTH
    off["gates"] = off["cq"] + MLA_Q_RANK
    return off


def kernel(x, c, ctx, c_ctx, w_mod, b_mod, g_norm1, g_norm2, w_in, na_q_gain, na_k_gain, na_rpb, na_w_o, gm_ln_g,
           gm_ln_b, gm_w_s, gm_b_s, gm_w_o, mla_cq_gain, mla_ckv_gain, mla_w_uq, mla_w_ukv, mla_q_gain, mla_k_gain,
           mla_w_o, w_out, ffn_w1, ffn_w2):
    batch, seq, d = x.shape
    ctx_len = ctx.shape[1]
    depth = w_mod.shape[0]
    grid_rows = seq // GRID_W
    assert N_BRANCHES * d == C_TOTAL - C_GATES and d == w_out.shape[-1] and batch + 1 <= MOD_ROWS
    assert seq % (NA_QR * GRID_W) == 0 and grid_rows >= NA_BAND and (grid_rows - NA_BAND) % NA_QR == 0
    assert seq % min(ROW_TILE, seq) == 0 and seq % min(MLA_TQ, seq) == 0 and seq % min(MLA_TK, seq) == 0
    assert ctx_len % GM_CHUNK == 0 and (batch * ctx_len) % min(CTX_ROW_TILE, batch * ctx_len) == 0
    p = dict(w_in=w_in, na_q_gain=na_q_gain, na_k_gain=na_k_gain, na_w_o=na_w_o, gm_ln_g=gm_ln_g, gm_ln_b=gm_ln_b,
             gm_w_s=gm_w_s, gm_b_s=gm_b_s, gm_w_o=gm_w_o, mla_cq_gain=mla_cq_gain, mla_ckv_gain=mla_ckv_gain,
             mla_w_uq=mla_w_uq, mla_w_ukv=mla_w_ukv, mla_q_gain=mla_q_gain, mla_k_gain=mla_k_gain, mla_w_o=mla_w_o,
             w_out=w_out, ffn_w1=ffn_w1, ffn_w2=ffn_w2)
    stacked = _pack_stacked(p, _in_offsets(d))

    mod_rows = MOD_ROWS
    ctx_row = batch
    rows = jnp.concatenate([c, c_ctx[None], jnp.zeros((mod_rows - batch - 1, d), F32)], axis=0)
    mod = _modulation(rows, w_mod, b_mod).reshape(depth, mod_rows, 6, 1, d)

    tm = min(ROW_TILE, seq)
    tmc = min(CTX_ROW_TILE, batch * ctx_len)
    cos_t, sin_t = _rope_tables(seq)
    cos_c = jnp.ones((tmc, LANES), F32)
    sin_c = jnp.zeros((tmc, LANES), F32)
    tiles_per_seq = seq // tm
    tiles_per_ctx = 1
    lat_row = lambda i: i // tiles_per_seq
    ctx_row_fn = lambda i: ctx_row
    na_bias = _na_bias_tables(na_rpb, grid_rows)

    xs = x.reshape(batch * seq, d)
    cs = ctx.reshape(batch * ctx_len, d)
    for i in range(depth):
        last = i == depth - 1
        lw = _pack_layer(i, p, stacked)
        mod_l = mod[i]
        gn1 = g_norm1[i][None]
        gn2 = g_norm2[i][None]

        (naq, nak, navt, gmu, gmv, gates, qm, km, vmt) = _in_projection(
            xs, mod_l, lat_row, tm, tiles_per_seq, gn1, lw, cos_t, sin_t)
        (naq_c, nak_c, navt_c, gmu_c, gmv_c, gates_c, qm_c, km_c, vmt_c) = _in_projection(
            cs, mod_l, ctx_row_fn, tmc, tiles_per_ctx, gn1, lw, cos_c, sin_c)

        y_na = _na_attention(naq, nak, navt, nak_c, navt_c, na_bias, i, batch, seq, ctx_len)
        y_mla = _mla_attention(qm, km_c, vmt_c, km, vmt, batch, seq, ctx_len, seq, tq=min(MLA_TQ, seq),
                               tk=min(MLA_TK, seq))
        xs_new = _merge_ffn(xs, y_na, y_mla, gmu, gmv, gates, mod_l, lat_row, tm, gn2, lw)

        if not last:
            y_na_c = _na_ctx_attention(naq_c, nak_c, navt_c, batch, ctx_len)
            y_mla_c = _mla_attention(qm_c, km_c, vmt_c, None, None, batch, ctx_len, ctx_len, 0, tq=ctx_len, tk=0,
                                     pairs_per_step=MLA_HEADS // 2)
            cs = _merge_ffn(cs, y_na_c, y_mla_c, gmu_c, gmv_c, gates_c, mod_l, ctx_row_fn, tmc, gn2, lw)
        xs = xs_new
    return xs.reshape(batch, seq, d)
```

```python
import functools
import math

import numpy as np
import jax
import jax.numpy as jnp
from jax import lax
from jax.experimental import pallas as pl
from jax.experimental.pallas import tpu as pltpu

F32 = jnp.float32
BF16 = jnp.bfloat16

LANES = 128
BF16_SUBLANES = 16
VMEM_LIMIT_BYTES = 60 * 1024 * 1024

GRID_W = 64
N_BRANCHES = 3
NA_HEADS = 8
NA_HEAD_DIM = 64
NA_WIDTH = NA_HEADS * NA_HEAD_DIM
NA_WIN_H = 8
NA_WIN_W = 16
GM_GROUPS = 4
GM_CHUNK = 128
GM_GROUP_DIM = 128
GM_WIDTH = GM_GROUPS * GM_GROUP_DIM
MLA_HEADS = 8
MLA_Q_RANK = 256
MLA_KV_RANK = 128
MLA_NOPE = 64
MLA_ROPE = 32
MLA_V = 64
MLA_QK = MLA_NOPE + MLA_ROPE
ROPE_AXIS_PAIRS = MLA_ROPE // 4
ROPE_THETA = 10000.0
NORM_EPS = 1e-6
NEG_INF = -1e30
LOG2E = math.log2(math.e)

NA_QR = 4
NA_BAND = 12
NA_BLOCKS_PER_STEP = 4
MLA_SLOT = LANES
ROW_TILE = 512
CTX_ROW_TILE = 512
MLA_TQ = 2048
MLA_TK = 256
MLA_CHAIN_Q = 256
GATE_CHUNK = 512
MOD_COL_TILE = 1536
MOD_ROWS = 16


def _resident(shape):
    zeros = (0,) * len(shape)
    return pl.BlockSpec(shape, lambda *_: zeros, pipeline_mode=pl.Buffered(1))


def _resident_layer(shape, layer):
    zeros = (0,) * len(shape)
    return pl.BlockSpec((None,) + tuple(shape), lambda *_: (layer,) + zeros, pipeline_mode=pl.Buffered(1))


def _params():
    return pltpu.CompilerParams(vmem_limit_bytes=VMEM_LIMIT_BYTES)


def _dot(a, b):
    return jnp.dot(a, b, preferred_element_type=F32)


def _dot_t(a, b):
    return lax.dot_general(a, b, (((1,), (1,)), ((), ())), preferred_element_type=F32)


def _mod_kernel(s_ref, w_ref, b_ref, o_ref):
    s = s_ref[...]
    s = s * jax.nn.sigmoid(s)
    o_ref[...] = _dot(s.astype(BF16), w_ref[...].astype(BF16)) + b_ref[...]


def _modulation(rows, w_mod, b_mod):
    depth, d, n = w_mod.shape
    r = rows.shape[0]
    tn = MOD_COL_TILE
    return pl.pallas_call(
        _mod_kernel,
        grid=(depth, n // tn),
        in_specs=[
            pl.BlockSpec((r, d), lambda l, j: (0, 0)),
            pl.BlockSpec((None, d, tn), lambda l, j: (l, 0, j)),
            pl.BlockSpec((None, 1, tn), lambda l, j: (l, 0, j)),
        ],
        out_specs=pl.BlockSpec((None, r, tn), lambda l, j: (l, 0, j)),
        out_shape=jax.ShapeDtypeStruct((depth, r, n), F32),
        compiler_params=_params(),
        name="modulation",
    )(rows, w_mod, b_mod.reshape(depth, 1, n))


C_NAQ = 0
C_NAK = C_NAQ + NA_WIDTH
C_CKV = C_NAK + NA_WIDTH
C_KR = C_CKV + MLA_KV_RANK
C_CQ = C_KR + LANES
C_GMU = C_CQ + MLA_Q_RANK
C_GMV = C_GMU + GM_WIDTH
C_GATES = C_GMV + GM_WIDTH
C_TOTAL = C_GATES + N_BRANCHES * 1024
assert MLA_NOPE + 2 * MLA_ROPE == MLA_SLOT == LANES


def _rms(t, width):
    return t * lax.rsqrt(jnp.sum(t * t, axis=-1, keepdims=True) * (1.0 / width) + NORM_EPS)


def _gelu(t):
    return 0.5 * t * (1.0 + lax.erf(t * math.sqrt(0.5)))


def _sigmoid(t):
    return 0.5 * jnp.tanh(0.5 * t) + 0.5


def _head64_norm(p, bd):
    p2 = (p * p).astype(BF16)
    half = 2 * LANES
    ss = jnp.concatenate([_dot(p2[:, :half], bd), _dot(p2[:, half:], bd)], axis=1)
    return p * lax.rsqrt(ss * (1.0 / NA_HEAD_DIM) + NORM_EPS)


def _inproj_kernel(x_ref, sh_ref, sc_ref, gn_ref, w_ref, wnavt_ref, qg_ref, kg_ref, bd_ref, ckvg_ref, cqg_ref, lng_ref,
                   lnb_ref, wq_ref, wqp_ref, gq_ref, gqp_ref, wk_ref, gk_ref, gkp_ref, wvt_ref, cos_ref,
                   sin_ref, *outs, keys_only):
    if keys_only:
        nak_ref, navt_ref, km_ref, vmt_ref = outs
    else:
        naq_ref, nak_ref, navt_ref, gmu_ref, gmv_ref, gates_ref, qm_ref, km_ref, vmt_ref = outs
    x = x_ref[...]
    d = x.shape[-1]
    h = _rms(x, d) * gn_ref[...]
    h = h * (1.0 + sc_ref[...]) + sh_ref[...]
    hb = h.astype(BF16)

    def proj(c0, width):
        return _dot(hb, w_ref[:, c0:c0 + width])

    p_ckv = proj(C_CKV, MLA_KV_RANK)
    p_cq = None if keys_only else proj(C_CQ, MLA_Q_RANK)
    kr = proj(C_KR, LANES)
    p_naq = None if keys_only else proj(C_NAQ, NA_WIDTH)
    p_nak = proj(C_NAK, NA_WIDTH)
    navt_ref[...] = _dot_t(wnavt_ref[...], hb).astype(BF16)

    ckv_n = (_rms(p_ckv, MLA_KV_RANK) * ckvg_ref[...]).astype(BF16)
    if not keys_only:
        cq_n = (_rms(p_cq, MLA_Q_RANK) * cqg_ref[...]).astype(BF16)

        p_gmu = proj(C_GMU, GM_WIDTH)
        p_gmv = proj(C_GMV, GM_WIDTH)

    bd = bd_ref[...]
    if not keys_only:
        naq_ref[...] = (_head64_norm(p_naq, bd) * qg_ref[...]).astype(BF16)
    nak_ref[...] = (_head64_norm(p_nak, bd) * kg_ref[...]).astype(BF16)

    vmt_ref[...] = _dot_t(wvt_ref[...], ckv_n).astype(BF16)
    if not keys_only:
        xq, xqp = _dot(cq_n, wq_ref[...]), _dot(cq_n, wqp_ref[...])
    xk = _dot(ckv_n, wk_ref[...])

    if not keys_only:
        gmu_ref[...] = _gelu(p_gmu).astype(BF16)
        gv = _gelu(p_gmv)
        mu = jnp.mean(gv, axis=-1, keepdims=True)
        gc = gv - mu
        var = jnp.mean(gc * gc, axis=-1, keepdims=True)
        gmv_ref[...] = (gc * lax.rsqrt(var + NORM_EPS) * lng_ref[...] + lnb_ref[...]).astype(BF16)

        n_gate = gates_ref.shape[-1]
        gate_chunk = GATE_CHUNK
        p_gate = proj(C_GATES, gate_chunk)

    cosv = cos_ref[...]
    sinv = sin_ref[...]

    def heads(xa, gcos, rotated, ss_shared, out_r):
        for hd in range(MLA_HEADS):
            sl = slice(hd * MLA_SLOT, (hd + 1) * MLA_SLOT)
            xh = xa[:, sl]
            ss = jnp.sum(xh * xh, axis=-1, keepdims=True) + ss_shared
            out_r[:, sl] = (lax.rsqrt(ss * (1.0 / MLA_QK) + NORM_EPS) * (xh * gcos + rotated(sl))).astype(BF16)

    if not keys_only:
        gsin_q = gqp_ref[...] * sinv
        heads(xq, gq_ref[...] * cosv, lambda sl: xqp[:, sl] * gsin_q, 0.0, qm_ref)
    gcos_k = gk_ref[...] * cosv
    lane = lax.broadcasted_iota(jnp.int32, (1, LANES), 1)
    kr_tail = jnp.where(lane < MLA_QK, kr, 0.0)
    kr_rot = kr_tail * gcos_k + pltpu.roll(kr, LANES - MLA_ROPE, 1) * (gkp_ref[...] * sinv)
    heads(xk, gcos_k, lambda sl: kr_rot, jnp.sum(kr_tail * kr_tail, axis=-1, keepdims=True), km_ref)

    for j in range(0 if keys_only else n_gate // gate_chunk):
        nxt = proj(C_GATES + (j + 1) * gate_chunk, gate_chunk) if (j + 1) * gate_chunk < n_gate else None
        gates_ref[:, j * gate_chunk:(j + 1) * gate_chunk] = _sigmoid(p_gate).astype(BF16)
        p_gate = nxt


def _in_projection(xs, mod_l, mod_row_fn, tm, tiles_per_seq, gn, lw, cos_t, sin_t, keys_only=False):
    r, d = xs.shape
    n_tiles = r // tm
    row = lambda w: pl.BlockSpec((tm, w), lambda i: (i, 0))
    modspec = lambda chunk: pl.BlockSpec((None, None, 1, d), lambda i: (mod_row_fn(i), chunk, 0, 0))
    pos = pl.BlockSpec((tm, LANES), lambda i: (i % tiles_per_seq, 0))
    in_specs = [
        row(d), modspec(0), modspec(1), _resident((1, d)), _resident_layer((d, C_TOTAL), lw["layer"]),
        _resident((NA_WIDTH, d)),
        _resident((1, NA_WIDTH)), _resident((1, NA_WIDTH)), _resident((2 * LANES, 2 * LANES)),
        _resident((1, MLA_KV_RANK)), _resident((1, MLA_Q_RANK)), _resident((1, GM_WIDTH)), _resident((1, GM_WIDTH)),
        _resident((MLA_Q_RANK, MLA_HEADS * MLA_SLOT)), _resident((MLA_Q_RANK, MLA_HEADS * MLA_SLOT)),
        _resident((1, MLA_SLOT)), _resident((1, MLA_SLOT)),
        _resident((MLA_KV_RANK, MLA_HEADS * MLA_SLOT)),
        _resident((1, MLA_SLOT)), _resident((1, MLA_SLOT)),
        _resident((MLA_HEADS * MLA_V, MLA_KV_RANK)), pos, pos,
    ]
    outs = [("naq", NA_WIDTH, False), ("nak", NA_WIDTH, False), ("navt", NA_WIDTH, True), ("gmu", GM_WIDTH, False),
            ("gmv", GM_WIDTH, False), ("gates", N_BRANCHES * d, False), ("qm", MLA_HEADS * MLA_SLOT, False),
            ("km", MLA_HEADS * MLA_SLOT, False), ("vmt", MLA_HEADS * MLA_V, True)]
    if keys_only:
        outs = [o for o in outs if o[0] in ("nak", "navt", "km", "vmt")]
    col = lambda w: pl.BlockSpec((w, tm), lambda i: (0, i))
    return pl.pallas_call(
        functools.partial(_inproj_kernel, keys_only=keys_only),
        grid=(n_tiles,),
        in_specs=in_specs,
        out_specs=[col(w) if t else row(w) for _, w, t in outs],
        out_shape=[jax.ShapeDtypeStruct((w, r) if t else (r, w), BF16) for _, w, t in outs],
        compiler_params=_params(),
        name="in_projection",
    )(xs, mod_l, mod_l, gn, lw["w_in"], lw["w_navt"], lw["na_qg"], lw["na_kg"], lw["bd"], lw["ckv_g"], lw["cq_g"],
      lw["ln_g"],
      lw["ln_b"], lw["wq"], lw["wqp"], lw["gq"], lw["gqp"], lw["wk"], lw["gk"], lw["gkp"], lw["wvt"],
      cos_t, sin_t)


def _pair_scores(q_pair, k_blocks):
    lane = lax.broadcasted_iota(jnp.int32, (1, LANES), 1)
    zero = jnp.zeros_like(q_pair)
    q2 = jnp.concatenate([jnp.where(lane < NA_HEAD_DIM, q_pair, zero), jnp.where(lane >= NA_HEAD_DIM, q_pair, zero)],
                         axis=0)
    return [_dot_t(k_pair, q2) if bias_t is None else _dot_t(k_pair, q2) + bias_t for k_pair, bias_t in k_blocks]


def _pair_softmax_pv(sts, vt_blocks):
    nq = sts[0].shape[1] // 2
    m = None
    for st in sts:
        mb = jnp.max(st, axis=0, keepdims=True)
        m = mb if m is None else jnp.maximum(m, mb)
    acc = None
    for st, vt_pair in zip(sts, vt_blocks):
        ones = jnp.ones((BF16_SUBLANES, vt_pair.shape[1]), BF16)
        part = _dot(jnp.concatenate([vt_pair, ones], axis=0), jnp.exp2(st - m).astype(BF16))
        acc = part if acc is None else acc + part
    o0 = acc[:NA_HEAD_DIM, :nq] / acc[LANES:LANES + 1, :nq]
    o1 = acc[NA_HEAD_DIM:LANES, nq:] / acc[LANES:LANES + 1, nq:]
    return jnp.concatenate([o0, o1], axis=0).T


def _pairs_pipelined(n_pairs, scores_fn, finish_fn):
    all_scores = [scores_fn(p) for p in range(n_pairs)]
    for p in range(n_pairs):
        finish_fn(p, all_scores[p])


def _na_kernel(q_ref, k_ref, vt_ref, kc_ref, vct_ref, bias_ref, o_ref, *, grid_rows, blocks_per_step):
    n_rb = grid_rows // NA_QR
    nk = NA_BAND * GRID_W
    tq = NA_QR * GRID_W
    n_pairs = NA_HEADS // 2
    cols = lambda hp: slice(hp * LANES, (hp + 1) * LANES)

    def geometry(sub):
        rb = pl.program_id(1) * blocks_per_step + sub
        cls = jnp.where(rb == 0, 0, jnp.where(rb == n_rb - 1, 2, 1))
        band0 = jnp.clip(NA_QR * rb - NA_WIN_H // 2, 0, grid_rows - NA_BAND)
        return cls, pl.multiple_of(band0 * GRID_W, 2 * LANES)

    geo = [geometry(sub) for sub in range(blocks_per_step)]

    def scores(c):
        sub, hp = divmod(c, n_pairs)
        cls, k0 = geo[sub]
        cs = cols(hp)
        k_all = jnp.concatenate([k_ref[pl.ds(k0, nk), cs], kc_ref[:, cs]], axis=0)
        st = _pair_scores(q_ref[sub * tq:(sub + 1) * tq, cs], [(k_all, None)])[0]
        return [jnp.concatenate([st[:nk] + bias_ref[cls, hp], st[nk:]], axis=0)]

    def finish(c, sts):
        sub, hp = divmod(c, n_pairs)
        _, k0 = geo[sub]
        cs = cols(hp)
        vt_all = jnp.concatenate([vt_ref[cs, pl.ds(k0, nk)], vct_ref[cs, :]], axis=1)
        o_ref[sub * tq:(sub + 1) * tq, cs] = _pair_softmax_pv(sts, [vt_all]).astype(BF16)

    nxt = [scores(p) for p in range(n_pairs)]
    for sub in range(blocks_per_step):
        cur = nxt
        if sub + 1 < blocks_per_step:
            nxt = [scores((sub + 1) * n_pairs + p) for p in range(n_pairs)]
        for p in range(n_pairs):
            finish(sub * n_pairs + p, cur[p])


def _na_attention(q, k, vt, kc, vct, bias, layer, batch, seq, ctx_len):
    grid_rows = seq // GRID_W
    blocks_per_step = math.gcd(NA_BLOCKS_PER_STEP, grid_rows // NA_QR)
    n_rb = grid_rows // NA_QR // blocks_per_step
    tq = NA_QR * GRID_W * blocks_per_step
    nk = NA_BAND * GRID_W

    return pl.pallas_call(
        functools.partial(_na_kernel, grid_rows=grid_rows, blocks_per_step=blocks_per_step),
        grid=(batch, n_rb),
        in_specs=[
            pl.BlockSpec((tq, NA_WIDTH), lambda b, rb: (b * n_rb + rb, 0)),
            pl.BlockSpec((seq, NA_WIDTH), lambda b, rb: (b, 0)),
            pl.BlockSpec((NA_WIDTH, seq), lambda b, rb: (0, b)),
            pl.BlockSpec((ctx_len, NA_WIDTH), lambda b, rb: (b, 0)),
            pl.BlockSpec((NA_WIDTH, ctx_len), lambda b, rb: (0, b)),
            pl.BlockSpec((None, 3, NA_HEADS // 2, nk, 2 * NA_QR * GRID_W), lambda b, rb: (layer, 0, 0, 0, 0),
                         pipeline_mode=pl.Buffered(1)),
        ],
        out_specs=pl.BlockSpec((tq, NA_WIDTH), lambda b, rb: (b * n_rb + rb, 0)),
        out_shape=jax.ShapeDtypeStruct((batch * seq, NA_WIDTH), BF16),
        compiler_params=_params(),
        name="na_attention",
    )(q, k, vt, kc, vct, bias)


NA_DR = 2 * NA_WIN_H - 1
NA_DC = 2 * NA_WIN_W - 1


def _na_window_geometry(grid_rows):
    n_rb = grid_rows // NA_QR
    kh, kw = NA_WIN_H, NA_WIN_W
    cq = np.arange(GRID_W)
    c0 = np.clip(cq - kw // 2, 0, GRID_W - kw)
    col_in = (cq[None, :] >= c0[:, None]) & (cq[None, :] < c0[:, None] + kw)
    dc = np.clip(cq[None, :] - cq[:, None], -(kw - 1), kw - 1) + kw - 1
    dc_t = np.tile(dc.T, (1, 2)).astype(np.int32)
    col_in_t = np.tile(col_in.T, (1, 2)).astype(np.int32)
    dr = np.zeros((3, NA_QR, NA_BAND), np.int32)
    ok = np.zeros((3, NA_QR, NA_BAND), bool)
    for c, rb in enumerate((0, 1, n_rb - 1)):
        band0 = int(np.clip(NA_QR * rb - kh // 2, 0, grid_rows - NA_BAND))
        for i in range(NA_QR):
            rq = NA_QR * rb + i
            r0 = int(np.clip(rq - kh // 2, 0, grid_rows - kh))
            for j in range(NA_BAND):
                kr = band0 + j
                ok[c, i, j] = r0 <= kr < r0 + kh
                dr[c, i, j] = int(np.clip(kr - rq + kh - 1, 0, NA_DR - 1))
    return dc_t, col_in_t, dr, ok


def _na_bias_kernel(rpb_ref, dc_ref, colin_ref, o_ref, tile_ref, *, dr, ok):
    layer = pl.program_id(0)
    dc = dc_ref[...]
    col_in = colin_ref[...] > 0
    n_tiles = NA_HEADS * NA_DR

    def build(t, carry):
        base = (layer * n_tiles + t) * NA_DC
        acc = jnp.zeros(dc.shape, F32)
        for d in range(NA_DC):
            acc = jnp.where(dc == d, rpb_ref[base + d], acc)
        tile_ref[t] = jnp.where(col_in, acc * LOG2E, NEG_INF)
        return carry

    lax.fori_loop(0, n_tiles, build, 0)

    lane = lax.broadcasted_iota(jnp.int32, (1, LANES), 1)
    neg = jnp.full((GRID_W, LANES), NEG_INF, F32)
    for c in range(3):
        for hp in range(NA_HEADS // 2):
            for j in range(NA_BAND):
                for g in range(2 * NA_QR // 2):
                    head = 2 * hp + g // (NA_QR // 2)
                    i0 = 2 * (g % (NA_QR // 2))
                    halves = [tile_ref[head * NA_DR + int(dr[c, i, j])] if ok[c, i, j] else neg for i in (i0, i0 + 1)]
                    o_ref[c, hp, j * GRID_W:(j + 1) * GRID_W, g * LANES:(g + 1) * LANES] = jnp.where(
                        lane < GRID_W, halves[0], halves[1])


def _na_bias_tables(na_rpb, grid_rows):
    depth = na_rpb.shape[0]
    dc_t, col_in_t, dr, ok = _na_window_geometry(grid_rows)
    nk, nq2 = NA_BAND * GRID_W, 2 * NA_QR * GRID_W
    return pl.pallas_call(
        functools.partial(_na_bias_kernel, dr=dr, ok=ok),
        grid=(depth,),
        in_specs=[
            pl.BlockSpec(memory_space=pltpu.SMEM),
            pl.BlockSpec((GRID_W, LANES), lambda l: (0, 0)),
            pl.BlockSpec((GRID_W, LANES), lambda l: (0, 0)),
        ],
        out_specs=pl.BlockSpec((None, 3, NA_HEADS // 2, nk, nq2), lambda l: (l, 0, 0, 0, 0)),
        out_shape=jax.ShapeDtypeStruct((depth, 3, NA_HEADS // 2, nk, nq2), F32),
        scratch_shapes=[pltpu.VMEM((NA_HEADS * NA_DR, GRID_W, LANES), F32)],
        compiler_params=_params(),
        name="na_bias_tables",
    )(na_rpb.reshape(-1), jnp.asarray(dc_t), jnp.asarray(col_in_t))


def _mla_kernel(q_ref, kc_ref, vct_ref, *rest, n_kb, tk):
    if n_kb:
        k_ref, vt_ref, o_ref = rest
    else:
        (o_ref,) = rest
    tq = q_ref.shape[0]
    pairs = q_ref.shape[1] // (2 * MLA_SLOT)
    slot = lambda pp, h: slice((2 * pp + h) * MLA_SLOT, (2 * pp + h + 1) * MLA_SLOT)
    pair_rows = lambda pp: slice(pp * 2 * MLA_V, (pp + 1) * 2 * MLA_V)
    lo = lax.broadcasted_iota(jnp.int32, (2 * MLA_V, 1), 0) < MLA_V

    def values(vt_blk):
        one = jnp.ones_like(vt_blk)
        return jnp.where(lo, vt_blk, one), jnp.where(lo, one, vt_blk)

    def update(st, m, acc, vt_aug):
        m_new = jnp.maximum(m, jnp.max(st, axis=0, keepdims=True))
        return m_new, jnp.exp2(m - m_new) * acc + _dot(vt_aug, jnp.exp2(st - m_new).astype(BF16))

    k_blk = lambda j, cs: kc_ref[:, cs] if j < 0 else k_ref[pl.ds(j * tk, tk), cs]
    vt_blk = lambda j, rs: vct_ref[rs, :] if j < 0 else vt_ref[rs, pl.ds(j * tk, tk)]
    blocks = list(range(-1, n_kb))
    nq = min(MLA_CHAIN_Q, tq)
    chains = [(pp, h, s) for pp in range(pairs) for s in range(tq // nq) for h in range(2)]
    q = [q_ref[s * nq:(s + 1) * nq, slot(pp, h)] for pp, h, s in chains]
    scores = lambda j, c: _dot_t(k_blk(j, slot(*chains[c][:2])), q[c])
    nxt = [scores(blocks[0], c) for c in range(len(chains))]
    m = [jnp.full((1, nq), NEG_INF, F32)] * len(chains)
    acc = [jnp.zeros((2 * MLA_V, nq), F32)] * len(chains)
    for idx, j in enumerate(blocks):
        vt_aug = [values(vt_blk(j, pair_rows(pp))) for pp in range(pairs)]
        for c, (pp, h, _) in enumerate(chains):
            st = nxt[c]
            if idx + 1 < len(blocks):
                nxt[c] = scores(blocks[idx + 1], c)
            m[c], acc[c] = update(st, m[c], acc[c], vt_aug[pp][h])
    for pp in range(pairs):
        acc_h = [jnp.concatenate([acc[c] for c, ch in enumerate(chains) if ch[:2] == (pp, hh)], axis=1)
                 for hh in range(2)]
        o0 = acc_h[0][:MLA_V] / acc_h[0][MLA_V:]
        o1 = acc_h[1][MLA_V:] / acc_h[1][:MLA_V]
        o_ref[:, pp * LANES:(pp + 1) * LANES] = jnp.concatenate([o0, o1], axis=0).T.astype(BF16)


def _mla_attention(q, kc, vct, k, vt, batch, q_len, ctx_len, kv_len, tq, tk, pairs_per_step=1):
    n_q = q_len // tq
    n_pairs = MLA_HEADS // 2 // pairs_per_step
    n_kb = kv_len // tk if k is not None else 0
    qk_w, v_w = pairs_per_step * 2 * MLA_SLOT, pairs_per_step * 2 * MLA_V
    in_specs = [
        pl.BlockSpec((tq, qk_w), lambda b, p, i: (b * n_q + i, p)),
        pl.BlockSpec((ctx_len, qk_w), lambda b, p, i: (b, p)),
        pl.BlockSpec((v_w, ctx_len), lambda b, p, i: (p, b)),
    ]
    args = [q, kc, vct]
    if n_kb:
        in_specs += [
            pl.BlockSpec((kv_len, qk_w), lambda b, p, i: (b, p)),
            pl.BlockSpec((v_w, kv_len), lambda b, p, i: (p, b)),
        ]
        args += [k, vt]
    return pl.pallas_call(
        functools.partial(_mla_kernel, n_kb=n_kb, tk=tk),
        grid=(batch, n_pairs, n_q),
        in_specs=in_specs,
        out_specs=pl.BlockSpec((tq, v_w), lambda b, p, i: (b * n_q + i, p)),
        out_shape=jax.ShapeDtypeStruct((batch * q_len, MLA_HEADS * MLA_V), BF16),
        compiler_params=_params(),
        name="mla_attention" if n_kb else "mla_ctx_attention",
    )(*args)


def _na_ctx_kernel(q_ref, k_ref, vt_ref, o_ref):
    cols = lambda hp: slice(hp * LANES, (hp + 1) * LANES)

    def scores(hp):
        return _pair_scores(q_ref[:, cols(hp)], [(k_ref[:, cols(hp)], None)])

    def finish(hp, sts):
        o_ref[:, cols(hp)] = _pair_softmax_pv(sts, [vt_ref[cols(hp), :]]).astype(BF16)

    _pairs_pipelined(NA_HEADS // 2, scores, finish)


def _na_ctx_attention(q, k, vt, batch, ctx_len):
    spec = pl.BlockSpec((ctx_len, NA_WIDTH), lambda b: (b, 0))
    return pl.pallas_call(
        _na_ctx_kernel,
        grid=(batch,),
        in_specs=[spec, spec, pl.BlockSpec((NA_WIDTH, ctx_len), lambda b: (0, b))],
        out_specs=spec,
        out_shape=jax.ShapeDtypeStruct((batch * ctx_len, NA_WIDTH), BF16),
        compiler_params=_params(),
        name="na_ctx_attention",
    )(q, k, vt)


FFN_HIDDEN_CHUNK = 512


def _merge_ffn_kernel(x_ref, yna_ref, ymla_ref, gmu_ref, gmv_ref, gates_ref, gt1_ref, sh2_ref, sc2_ref, gt2_ref,
                      gn2_ref, wna_ref, wgm_ref, wmla_ref, wout_ref, w1_ref, w2_ref, ws_ref, bsb_ref, o_ref,
                      ygm_ref):
    tm, d = x_ref.shape
    for c in range(tm // GM_CHUNK):
        rs = slice(c * GM_CHUNK, (c + 1) * GM_CHUNK)
        for g in range(GM_GROUPS):
            cs = slice(g * GM_GROUP_DIM, (g + 1) * GM_GROUP_DIM)
            mixed = _dot(ws_ref[g], gmv_ref[rs, cs]) + bsb_ref[:, cs]
            ygm_ref[rs, cs] = (gmu_ref[rs, cs].astype(F32) * mixed).astype(BF16)

    y = gates_ref[:, 0:d].astype(F32) * _dot(yna_ref[...], wna_ref[...])
    y = y + gates_ref[:, d:2 * d].astype(F32) * _dot(ygm_ref[...], wgm_ref[...])
    y = y + gates_ref[:, 2 * d:3 * d].astype(F32) * _dot(ymla_ref[...], wmla_ref[...])
    xn = x_ref[...] + gt1_ref[...] * _dot(y.astype(BF16), wout_ref[...])

    h2 = _rms(xn, d) * gn2_ref[...]
    h2 = (h2 * (1.0 + sc2_ref[...]) + sh2_ref[...]).astype(BF16)
    hidden = w1_ref.shape[1]
    acc = None
    for c in range(hidden // FFN_HIDDEN_CHUNK):
        cs = slice(c * FFN_HIDDEN_CHUNK, (c + 1) * FFN_HIDDEN_CHUNK)
        a = jnp.maximum(_dot(h2, w1_ref[:, cs]), 0.0)
        part = _dot((a * a).astype(BF16), w2_ref[cs, :])
        acc = part if acc is None else acc + part
    o_ref[...] = xn + gt2_ref[...] * acc


def _merge_ffn(xs, y_na, y_mla, gm_u, gm_v, gates, mod_l, mod_row_fn, tm, gn2, lw):
    r, d = xs.shape
    hidden = lw["w1"].shape[-1]
    wl = lambda shape: _resident_layer(shape, lw["layer"])
    row = lambda w: pl.BlockSpec((tm, w), lambda i: (i, 0))
    modspec = lambda chunk: pl.BlockSpec((None, None, 1, d), lambda i: (mod_row_fn(i), chunk, 0, 0))
    in_specs = [
        row(d), row(NA_WIDTH), row(MLA_HEADS * MLA_V), row(GM_WIDTH), row(GM_WIDTH), row(N_BRANCHES * d),
        modspec(2), modspec(3), modspec(4), modspec(5), _resident((1, d)),
        wl((NA_WIDTH, d)), wl((GM_WIDTH, d)), wl((MLA_HEADS * MLA_V, d)), wl((d, d)),
        wl((d, hidden)), wl((hidden, d)), _resident((GM_GROUPS, GM_CHUNK, GM_CHUNK)),
        _resident((GM_CHUNK, GM_WIDTH)),
    ]
    return pl.pallas_call(
        _merge_ffn_kernel,
        grid=(r // tm,),
        in_specs=in_specs,
        out_specs=row(d),
        out_shape=jax.ShapeDtypeStruct((r, d), F32),
        scratch_shapes=[pltpu.VMEM((tm, GM_WIDTH), BF16)],
        compiler_params=_params(),
        name="merge_ffn",
    )(xs, y_na, y_mla, gm_u, gm_v, gates, mod_l, mod_l, mod_l, mod_l, gn2, lw["w_na_o"], lw["w_gm_o"],
      lw["w_mla_o"], lw["w_out"], lw["w1"], lw["w2"], lw["w_s"], lw["b_s"])


def _rope_partner():
    perm = np.arange(MLA_QK)
    for u in range(MLA_ROPE):
        seg, half, pair = u // (2 * ROPE_AXIS_PAIRS), (u // ROPE_AXIS_PAIRS) % 2, u % ROPE_AXIS_PAIRS
        perm[MLA_NOPE + u] = MLA_NOPE + seg * 2 * ROPE_AXIS_PAIRS + (1 - half) * ROPE_AXIS_PAIRS + pair
    tail = np.arange(MLA_QK) >= MLA_NOPE
    return perm, tail


def _pad_heads(t):
    pad = [(0, 0)] * (t.ndim - 1) + [(0, MLA_SLOT - MLA_QK)]
    t = jnp.pad(t, pad)
    return t.reshape(t.shape[:-2] + (MLA_HEADS * MLA_SLOT,))


def _rope_tables(seq):
    t = jnp.arange(seq)
    rows = (t // GRID_W).astype(F32)
    colsv = (t % GRID_W).astype(F32)
    freqs = ROPE_THETA ** (-jnp.arange(ROPE_AXIS_PAIRS, dtype=F32) / ROPE_AXIS_PAIRS)
    ang = jnp.stack([rows[:, None] * freqs, colsv[:, None] * freqs], axis=1)
    cos, sin = jnp.cos(ang), jnp.sin(ang)
    cos_t = jnp.stack([cos, cos], axis=2).reshape(seq, MLA_ROPE)
    sin_t = jnp.stack([-sin, sin], axis=2).reshape(seq, MLA_ROPE)
    ones = jnp.ones((seq, MLA_NOPE), F32)
    padc = jnp.ones((seq, MLA_SLOT - MLA_QK), F32)
    cos_full = jnp.concatenate([ones, cos_t, padc], axis=1)
    sin_full = jnp.concatenate([0 * ones, sin_t, 0 * padc], axis=1)
    return cos_full, sin_full


def _pack_stacked(p, off):
    w = p["w_in"].astype(BF16)
    depth, d = w.shape[0], w.shape[1]
    cols = lambda name, width: w[:, :, off[name]:off[name] + width]
    kr = cols("kr", MLA_ROPE)
    perm, _ = _rope_partner()
    w_in = jnp.concatenate([
        cols("naq", NA_WIDTH), cols("nak", NA_WIDTH), cols("ckv", MLA_KV_RANK),
        jnp.zeros((depth, d, MLA_NOPE), BF16), kr, kr[:, :, perm[MLA_NOPE:] - MLA_NOPE],
        cols("cq", MLA_Q_RANK), cols("gmu", GM_WIDTH), cols("gmv", GM_WIDTH), cols("gates", N_BRANCHES * d),
    ], axis=2)
    return {
        "w_in": w_in, "w_nav": cols("nav", NA_WIDTH),
        "w_na_o": p["na_w_o"].astype(BF16), "w_gm_o": p["gm_w_o"].astype(BF16),
        "w_mla_o": p["mla_w_o"].astype(BF16), "w_out": p["w_out"].astype(BF16),
        "w1": p["ffn_w1"].astype(BF16), "w2": p["ffn_w2"].astype(BF16),
    }


def _pack_layer(i, p, stacked):
    perm, tail = _rope_partner()
    q_scale = MLA_QK ** -0.5 * LOG2E
    w_uq = p["mla_w_uq"][i].reshape(MLA_Q_RANK, MLA_HEADS, MLA_QK)
    wq = _pad_heads(w_uq).astype(BF16)
    wqp = _pad_heads(w_uq[:, :, perm] * tail).astype(BF16)
    qg = p["mla_q_gain"][i]
    gq = jnp.pad(qg, (0, MLA_SLOT - MLA_QK))[None] * q_scale
    gqp = jnp.pad(qg[perm] * tail, (0, MLA_SLOT - MLA_QK))[None] * q_scale

    w_ukv = p["mla_w_ukv"][i].reshape(MLA_KV_RANK, MLA_HEADS, MLA_NOPE + MLA_V)
    k_nope = jnp.pad(w_ukv[:, :, :MLA_NOPE], ((0, 0), (0, 0), (0, MLA_SLOT - MLA_NOPE)))
    wk = k_nope.reshape(MLA_KV_RANK, MLA_HEADS * MLA_SLOT).astype(BF16)
    kg = p["mla_k_gain"][i]
    gk = jnp.pad(kg, (0, MLA_SLOT - MLA_QK))[None]
    gkp = jnp.pad(kg[perm] * tail, (0, MLA_SLOT - MLA_QK))[None]
    wvt = w_ukv[:, :, MLA_NOPE:].reshape(MLA_KV_RANK, MLA_HEADS * MLA_V).T.astype(BF16)

    blk = np.arange(2 * LANES) // NA_HEAD_DIM
    bd = jnp.asarray(blk[:, None] == blk[None, :], BF16)

    return {
        **stacked, "layer": i,
        "w_navt": stacked["w_nav"][i].T,
        "na_qg": jnp.tile(p["na_q_gain"][i], NA_HEADS)[None] * (NA_HEAD_DIM ** -0.5 * LOG2E),
        "na_kg": jnp.tile(p["na_k_gain"][i], NA_HEADS)[None],
        "bd": bd,
        "ckv_g": p["mla_ckv_gain"][i][None], "cq_g": p["mla_cq_gain"][i][None],
        "ln_g": p["gm_ln_g"][i][None], "ln_b": p["gm_ln_b"][i][None],
        "wq": wq, "wqp": wqp, "gq": gq, "gqp": gqp, "wk": wk, "gk": gk, "gkp": gkp, "wvt": wvt,
        "w_s": p["gm_w_s"][i].astype(BF16),
        "b_s": jnp.repeat(p["gm_b_s"][i].T, GM_GROUP_DIM, axis=1),
    }


def _in_offsets(d):
    off = {"nak": 0}
    off["nav"] = off["nak"] + NA_WIDTH
    off["ckv"] = off["nav"] + NA_WIDTH
    off["kr"] = off["ckv"] + MLA_KV_RANK
    off["naq"] = off["kr"] + MLA_ROPE
    off["gmu"] = off["naq"] + NA_WIDTH
    off["gmv"] = off["gmu"] + GM_WIDTH
    off["cq"] = off["gmv"] + GM_WIDTH
    off["gates"] = off["cq"] + MLA_Q_RANK
    return off


def kernel(x, c, ctx, c_ctx, w_mod, b_mod, g_norm1, g_norm2, w_in, na_q_gain, na_k_gain, na_rpb, na_w_o, gm_ln_g,
           gm_ln_b, gm_w_s, gm_b_s, gm_w_o, mla_cq_gain, mla_ckv_gain, mla_w_uq, mla_w_ukv, mla_q_gain, mla_k_gain,
           mla_w_o, w_out, ffn_w1, ffn_w2):
    batch, seq, d = x.shape
    ctx_len = ctx.shape[1]
    depth = w_mod.shape[0]
    grid_rows = seq // GRID_W
    assert N_BRANCHES * d == C_TOTAL - C_GATES and d == w_out.shape[-1] and batch + 1 <= MOD_ROWS
    assert seq % (NA_QR * GRID_W) == 0 and grid_rows >= NA_BAND and (grid_rows - NA_BAND) % NA_QR == 0
    assert seq % min(ROW_TILE, seq) == 0 and seq % min(MLA_TQ, seq) == 0 and seq % min(MLA_TK, seq) == 0
    assert ctx_len % GM_CHUNK == 0 and (batch * ctx_len) % min(CTX_ROW_TILE, batch * ctx_len) == 0
    p = dict(w_in=w_in, na_q_gain=na_q_gain, na_k_gain=na_k_gain, na_w_o=na_w_o, gm_ln_g=gm_ln_g, gm_ln_b=gm_ln_b,
             gm_w_s=gm_w_s, gm_b_s=gm_b_s, gm_w_o=gm_w_o, mla_cq_gain=mla_cq_gain, mla_ckv_gain=mla_ckv_gain,
             mla_w_uq=mla_w_uq, mla_w_ukv=mla_w_ukv, mla_q_gain=mla_q_gain, mla_k_gain=mla_k_gain, mla_w_o=mla_w_o,
             w_out=w_out, ffn_w1=ffn_w1, ffn_w2=ffn_w2)
    stacked = _pack_stacked(p, _in_offsets(d))

    mod_rows = MOD_ROWS
    ctx_row = batch
    rows = jnp.concatenate([c, c_ctx[None], jnp.zeros((mod_rows - batch - 1, d), F32)], axis=0)
    mod = _modulation(rows, w_mod, b_mod).reshape(depth, mod_rows, 6, 1, d)

    tm = min(ROW_TILE, seq)
    tmc = min(CTX_ROW_TILE, batch * ctx_len)
    cos_t, sin_t = _rope_tables(seq)
    cos_c = jnp.ones((tmc, LANES), F32)
    sin_c = jnp.zeros((tmc, LANES), F32)
    tiles_per_seq = seq // tm
    tiles_per_ctx = 1
    lat_row = lambda i: i // tiles_per_seq
    ctx_row_fn = lambda i: ctx_row
    na_bias = _na_bias_tables(na_rpb, grid_rows)

    xs = x.reshape(batch * seq, d)
    cs = ctx.reshape(batch * ctx_len, d)
    for i in range(depth):
        last = i == depth - 1
        lw = _pack_layer(i, p, stacked)
        mod_l = mod[i]
        gn1 = g_norm1[i][None]
        gn2 = g_norm2[i][None]

        (naq, nak, navt, gmu, gmv, gates, qm, km, vmt) = _in_projection(
            xs, mod_l, lat_row, tm, tiles_per_seq, gn1, lw, cos_t, sin_t)
        if last:
            nak_c, navt_c, km_c, vmt_c = _in_projection(
                cs, mod_l, ctx_row_fn, tmc, tiles_per_ctx, gn1, lw, cos_c, sin_c, keys_only=True)
        else:
            (naq_c, nak_c, navt_c, gmu_c, gmv_c, gates_c, qm_c, km_c, vmt_c) = _in_projection(
                cs, mod_l, ctx_row_fn, tmc, tiles_per_ctx, gn1, lw, cos_c, sin_c)

        y_na = _na_attention(naq, nak, navt, nak_c, navt_c, na_bias, i, batch, seq, ctx_len)
        y_mla = _mla_attention(qm, km_c, vmt_c, km, vmt, batch, seq, ctx_len, seq, tq=min(MLA_TQ, seq),
                               tk=min(MLA_TK, seq))
        xs_new = _merge_ffn(xs, y_na, y_mla, gmu, gmv, gates, mod_l, lat_row, tm, gn2, lw)

        if not last:
            y_na_c = _na_ctx_attention(naq_c, nak_c, navt_c, batch, ctx_len)
            y_mla_c = _mla_attention(qm_c, km_c, vmt_c, None, None, batch, ctx_len, ctx_len, 0, tq=ctx_len, tk=0,
                                     pairs_per_step=MLA_HEADS // 2)
            cs = _merge_ffn(cs, y_na_c, y_mla_c, gmu_c, gmv_c, gates_c, mod_l, ctx_row_fn, tmc, gn2, lw)
        xs = xs_new
    return xs.reshape(batch, seq, d)
```
